```python
import math
import jax
import jax.numpy as jnp
from jax import lax
import numpy as np

D_MODEL = 1024
BATCH = 2
SEQ = 8192
DEPTH = 4
DEC_BATCH = 32
DEC_SEQ = 4
PAST_LEN = 8192
PAGE_SIZE = 128

N_EVEN = (DEPTH + 1) // 2
N_ODD = DEPTH // 2
MIX = D_MODEL
A_GROUPS = 4
A_WIDTH = MIX // 2
A_GD = A_WIDTH // A_GROUPS
CHUNK = 128
POOL_WINDOWS = (2, 4, 8, 16)
B_GROUPS = len(POOL_WINDOWS)
B_WIDTH = MIX - A_WIDTH
B_GD = B_WIDTH // B_GROUPS
POOL_BUF = max(POOL_WINDOWS) - 1
C_HEADS = 8
C_HD = 64
C_WIDTH = C_HEADS * C_HD
Q_BLOCK = 128
SB_BIAS_INIT = -8.0
D_WIDTH = MIX - C_WIDTH
CONV_W = 3
FF_DENSE = 2816
N_EXPERTS = 8
TOP_K = 2
FF_EXPERT = 3584
EPS = 1e-6

kernel_name = 'hybrid_gmlp_pool_stickbreak_shortconv_step'


def rmsnorm(x, g):
    xf = x.astype(jnp.float32)
    y = xf * lax.rsqrt(jnp.mean(xf * xf, axis=-1, keepdims=True) + EPS)
    return (y * g.astype(jnp.float32)).astype(x.dtype)


def chunk_gating(u, v, w_s, b_s):
    bsz, t, g, c = v.shape
    L = t if t <= CHUNK else CHUNK
    n = t // L
    mask = jnp.tril(jnp.ones((L, L), dtype=v.dtype))
    w = w_s[:, :L, :L] * mask
    vc = v.reshape(bsz, n, L, g, c)
    s = jnp.einsum('gts,bnsgc->bntgc', w, vc) + b_s[:, :L].T[None, None, :, :, None]
    return (u.reshape(bsz, n, L, g, c) * s).reshape(bsz, t, g * c)


def multi_pool(z_ext, n_prefix, start_pos, w_pool, scale):
    bsz, pt, g, c = z_ext.shape
    t = pt - n_prefix
    zf = z_ext.astype(jnp.float32)
    cs = jnp.concatenate([jnp.zeros((bsz, 1, g, c), jnp.float32), jnp.cumsum(zf, axis=1)], axis=1)
    i = jnp.arange(t)
    hi = n_prefix + i + 1
    pos = start_pos + i
    means = []
    for gi, w in enumerate(POOL_WINDOWS):
        lo = jnp.maximum(hi - w, 0)
        cnt = jnp.minimum(pos + 1, w).astype(jnp.float32)
        means.append((cs[:, hi, gi] - cs[:, lo, gi]) / cnt[None, :, None])
    pooled = jnp.stack(means, axis=2) - zf[:, n_prefix:]
    y = jnp.einsum('btgc,gcd->btgd', pooled, w_pool.astype(jnp.float32))
    return (y.reshape(bsz, t, g * c) * scale.astype(jnp.float32)).astype(z_ext.dtype)


def stick_breaking(q, k, v, q_pos, k_pos, bias):
    qf = q.astype(jnp.float32)
    kf = k.astype(jnp.float32)
    z = jnp.einsum('bqhd,bkhd->bhqk', qf, kf) * (C_HD ** -0.5) + bias.astype(jnp.float32)[None, :, None, None]
    mask = (k_pos[None, :] < q_pos[:, None])[None, None]
    log_beta = jax.nn.log_sigmoid(z)
    log_rem = jnp.where(mask, jax.nn.log_sigmoid(-z), 0.0)
    later = lax.cumsum(log_rem, axis=3, reverse=True) - log_rem
    w = jnp.where(mask, jnp.exp(log_beta + later), 0.0)
    out = jnp.einsum('bhqk,bkhd->bqhd', w, v.astype(jnp.float32))
    return out.astype(v.dtype)


def sb_prompt(q, k, v, bias):
    bsz, t, h, d = q.shape
    k_pos = jnp.arange(t)
    if t <= Q_BLOCK or t % Q_BLOCK != 0:
        return stick_breaking(q, k, v, k_pos, k_pos, bias)
    nb = t // Q_BLOCK
    qb = q.reshape(bsz, nb, Q_BLOCK, h, d).swapaxes(0, 1)
    pb = k_pos.reshape(nb, Q_BLOCK)
    out = lax.map(lambda a: stick_breaking(a[0], k, v, a[1], k_pos, bias), (qb, pb))
    return out.swapaxes(0, 1).reshape(bsz, t, h, d)


def even_mixer(h, pool_prefix, start_pos, w_in, w_out, v_g, w_s, b_s, w_pool, pool_scale):
    bsz, t, _ = h.shape
    p = h @ w_in
    u = p[..., :A_WIDTH].reshape(bsz, t, A_GROUPS, A_GD)
    v = rmsnorm(p[..., A_WIDTH:2 * A_WIDTH].reshape(bsz, t, A_GROUPS, A_GD), v_g.reshape(A_GROUPS, A_GD))
    z = p[..., 2 * A_WIDTH:]
    a = chunk_gating(u, v, w_s, b_s)
    z_ext = jnp.concatenate([pool_prefix.astype(z.dtype), z], axis=1)
    b = multi_pool(z_ext.reshape(bsz, -1, B_GROUPS, B_GD), pool_prefix.shape[1], start_pos, w_pool, pool_scale)
    y = jnp.concatenate([a, b], axis=-1) @ w_out
    return y, v.reshape(bsz, t, A_WIDTH), z_ext[:, -POOL_BUF:]


def odd_mixer(h, k_past, v_past, conv_prefix, start_pos, w_in, w_out, q_g, k_g, sb_b, cw):
    bsz, t, _ = h.shape
    p = h @ w_in
    q, k, v, gb, gc, hx = jnp.split(p, [C_WIDTH, 2 * C_WIDTH, 3 * C_WIDTH, 3 * C_WIDTH + D_WIDTH, 3 * C_WIDTH + 2 * D_WIDTH], axis=-1)
    q = rmsnorm(q.reshape(bsz, t, C_HEADS, C_HD), q_g)
    k = rmsnorm(k.reshape(bsz, t, C_HEADS, C_HD), k_g)
    v = v.reshape(bsz, t, C_HEADS, C_HD)
    if k_past is None:
        att = sb_prompt(q, k, v, sb_b)
    else:
        k_all = jnp.concatenate([k_past.astype(k.dtype), k], axis=1)
        v_all = jnp.concatenate([v_past.astype(v.dtype), v], axis=1)
        att = stick_breaking(q, k_all, v_all, start_pos + jnp.arange(t), jnp.arange(k_all.shape[1]), sb_b)
    zc = gc * hx
    z_ext = jnp.concatenate([conv_prefix.astype(zc.dtype), zc], axis=1)
    conv = z_ext[:, 0:t] * cw[0]
    for j in range(1, CONV_W):
        conv = conv + z_ext[:, j:j + t] * cw[j]
    d = gb * conv
    y = jnp.concatenate([att.reshape(bsz, t, C_WIDTH), d], axis=-1) @ w_out
    return y, k, v, z_ext[:, -(CONV_W - 1):]


def swiglu(h, wg, wu, wd):
    return (jax.nn.silu(h @ wg) * (h @ wu)) @ wd


def moe(h, router, wg, wu, wd):
    logits = (h @ router).astype(jnp.float32)
    top_v, top_i = lax.top_k(logits, TOP_K)
    gates = jax.nn.softmax(top_v, axis=-1)
    dense_gate = jnp.einsum('btk,btke->bte', gates, jax.nn.one_hot(top_i, N_EXPERTS, dtype=jnp.float32)).astype(h.dtype)
    out = jnp.zeros(h.shape[:-1] + (wd.shape[-1],), h.dtype)
    for e in range(N_EXPERTS):
        out = out + dense_gate[..., e:e + 1] * swiglu(h, wg[e], wu[e], wd[e])
    return out


def setup_inputs(seed: int = 0) -> dict:
    key = jax.random.key(seed)
    ks = jax.random.split(key, 32)
    f32 = jnp.float32
    n_pages = PAST_LEN // PAGE_SIZE
    n_pool = (DEC_BATCH * n_pages * 5) // 4

    def nrm(k, shape, fan_in):
        return jax.random.normal(k, shape, f32) * (fan_in ** -0.5)

    def gain(k, shape, s=0.05):
        return 1.0 + s * jax.random.normal(k, shape, f32)

    perm = jax.random.permutation(ks[4], n_pool)
    page_table = perm[: DEC_BATCH * n_pages].reshape(DEC_BATCH, n_pages).astype(jnp.int32)
    return {
        'x_prompt': jax.random.normal(ks[0], (BATCH, SEQ, D_MODEL), f32),
        'x_sample': jax.random.normal(ks[1], (DEC_BATCH, DEC_SEQ, D_MODEL), f32),
        'cache_k': jax.random.normal(ks[2], (N_ODD, n_pool, PAGE_SIZE, C_HEADS, C_HD), f32) * 0.5,
        'cache_v': jax.random.normal(ks[3], (N_ODD, n_pool, PAGE_SIZE, C_HEADS, C_HD), f32),
        'page_table': page_table,
        'state_pool': jax.random.normal(ks[5], (N_EVEN, DEC_BATCH, POOL_BUF, B_WIDTH), f32),
        'state_conv': jax.random.normal(ks[6], (N_ODD, DEC_BATCH, CONV_W - 1, D_WIDTH), f32),
        'norm_mix': gain(ks[7], (DEPTH, D_MODEL)),
        'norm_ffn': gain(ks[8], (DEPTH, D_MODEL)),
        'w_in_even': nrm(ks[9], (N_EVEN, D_MODEL, 2 * A_WIDTH + B_WIDTH), D_MODEL),
        'w_out_even': nrm(ks[10], (N_EVEN, A_WIDTH + B_WIDTH, D_MODEL), A_WIDTH + B_WIDTH),
        'v_norm_g': gain(ks[11], (N_EVEN, A_WIDTH)),
        'w_spatial': nrm(ks[12], (N_EVEN, A_GROUPS, CHUNK, CHUNK), CHUNK),
        'b_spatial': gain(ks[13], (N_EVEN, A_GROUPS, CHUNK)),
        'w_pool': nrm(ks[14], (N_EVEN, B_GROUPS, B_GD, B_GD), B_GD),
        'pool_scale': gain(ks[15], (N_EVEN, B_WIDTH), 0.1),
        'w_in_odd': nrm(ks[16], (N_ODD, D_MODEL, 3 * C_WIDTH + 3 * D_WIDTH), D_MODEL),
        'w_out_odd': nrm(ks[17], (N_ODD, C_WIDTH + D_WIDTH, D_MODEL), C_WIDTH + D_WIDTH),
        'q_norm_g': 0.5 * gain(ks[18], (N_ODD, C_HD)),
        'k_norm_g': 0.5 * gain(ks[19], (N_ODD, C_HD)),
        'sb_bias': SB_BIAS_INIT + 0.1 * jax.random.normal(ks[28], (N_ODD, C_HEADS), f32),
        'conv_w': nrm(ks[20], (N_ODD, CONV_W, D_WIDTH), CONV_W),
        'w_gate_dense': nrm(ks[21], (N_EVEN, D_MODEL, FF_DENSE), D_MODEL),
        'w_up_dense': nrm(ks[22], (N_EVEN, D_MODEL, FF_DENSE), D_MODEL),
        'w_down_dense': nrm(ks[23], (N_EVEN, FF_DENSE, D_MODEL), FF_DENSE),
        'w_router': nrm(ks[24], (N_ODD, D_MODEL, N_EXPERTS), D_MODEL),
        'w_gate_exp': nrm(ks[25], (N_ODD, N_EXPERTS, D_MODEL, FF_EXPERT), D_MODEL),
        'w_up_exp': nrm(ks[26], (N_ODD, N_EXPERTS, D_MODEL, FF_EXPERT), D_MODEL),
        'w_down_exp': nrm(ks[27], (N_ODD, N_EXPERTS, FF_EXPERT, D_MODEL), FF_EXPERT),
    }


def reference(x_prompt, x_sample, cache_k, cache_v, page_table, state_pool, state_conv, norm_mix, norm_ffn,
              w_in_even, w_out_even, v_norm_g, w_spatial, b_spatial, w_pool, pool_scale,
              w_in_odd, w_out_odd, q_norm_g, k_norm_g, sb_bias, conv_w,
              w_gate_dense, w_up_dense, w_down_dense, w_router, w_gate_exp, w_up_exp, w_down_exp):
    n_pages = PAST_LEN // PAGE_SIZE
    xp, xs = x_prompt, x_sample
    bp, bs = xp.shape[0], xs.shape[0]
    kp_l, vp_l, ks_l, vs_l = [], [], [], []
    poolp_l, pools_l, convp_l, convs_l, chunkv_l = [], [], [], [], []
    for layer in range(DEPTH):
        i = layer // 2
        hp = rmsnorm(xp, norm_mix[layer])
        hs = rmsnorm(xs, norm_mix[layer])
        if layer % 2 == 0:
            ew = (w_in_even[i], w_out_even[i], v_norm_g[i], w_spatial[i], b_spatial[i], w_pool[i], pool_scale[i])
            yp, _, pool_p = even_mixer(hp, jnp.zeros((bp, 0, B_WIDTH), hp.dtype), 0, *ew)
            ys, v_s, pool_s = even_mixer(hs, state_pool[i], PAST_LEN, *ew)
            poolp_l.append(pool_p)
            pools_l.append(pool_s)
            chunkv_l.append(v_s)
        else:
            ow = (w_in_odd[i], w_out_odd[i], q_norm_g[i], k_norm_g[i], sb_bias[i], conv_w[i])
            yp, k_p, v_p, conv_p = odd_mixer(hp, None, None, jnp.zeros((bp, CONV_W - 1, D_WIDTH), hp.dtype), 0, *ow)
            k_past = cache_k[i][page_table].reshape(bs, n_pages * PAGE_SIZE, C_HEADS, C_HD)
            v_past = cache_v[i][page_table].reshape(bs, n_pages * PAGE_SIZE, C_HEADS, C_HD)
            ys, k_s, v_s, conv_s = odd_mixer(hs, k_past, v_past, state_conv[i], PAST_LEN, *ow)
            kp_l.append(k_p)
            vp_l.append(v_p)
            ks_l.append(k_s)
            vs_l.append(v_s)
            convp_l.append(conv_p)
            convs_l.append(conv_s)
        xp = xp + yp
        xs = xs + ys
        hp = rmsnorm(xp, norm_ffn[layer])
        hs = rmsnorm(xs, norm_ffn[layer])
        if layer % 2 == 0:
            xp = xp + swiglu(hp, w_gate_dense[i], w_up_dense[i], w_down_dense[i])
            xs = xs + swiglu(hs, w_gate_dense[i], w_up_dense[i], w_down_dense[i])
        else:
            xp = xp + moe(hp, w_router[i], w_gate_exp[i], w_up_exp[i], w_down_exp[i])
            xs = xs + moe(hs, w_router[i], w_gate_exp[i], w_up_exp[i], w_down_exp[i])
    return (xp, xs, jnp.stack(kp_l), jnp.stack(vp_l), jnp.stack(ks_l), jnp.stack(vs_l),
            jnp.stack(poolp_l), jnp.stack(pools_l), jnp.stack(convp_l), jnp.stack(convs_l), jnp.stack(chunkv_l))
```

```python
import functools

import numpy as np
import jax
import jax.numpy as jnp
from jax import lax
from jax.experimental import pallas as pl
from jax.experimental.pallas import tpu as pltpu

F32 = jnp.float32
BF16 = jnp.bfloat16

D_MODEL = 1024
SEQ = 8192
DEPTH = 4
DEC_BATCH = 32
DEC_SEQ = 4
PAST_LEN = 8192
PAGE_SIZE = 128
N_PAGES = PAST_LEN // PAGE_SIZE
A_GROUPS = 4
A_WIDTH = 512
CHUNK = 128
POOL_WINDOWS = (2, 4, 8, 16)
B_WIDTH = 512
POOL_BUF = 15
C_HEADS = 8
C_HD = 64
C_WIDTH = 512
D_WIDTH = 512
CONV_W = 3
FF_DENSE = 2816
N_EXPERTS = 8
TOP_K = 2
FF_EXPERT = 3584
EPS = 1e-6

LANES = 128
ROW_TILE = 512
ATT_TQ = 256
PAGES_PER_STEP = 8
MOE_TM = 512
MOE_TF = 512
VMEM_LIMIT = 56 * 1024 * 1024


def _cparams(sem):
    return pltpu.CompilerParams(dimension_semantics=sem, vmem_limit_bytes=VMEM_LIMIT)


def _rms(x, g):
    return x * lax.rsqrt(jnp.mean(x * x, axis=-1, keepdims=True) + EPS) * g


def _split2(x):
    hi = x.astype(BF16)
    lo = (x - hi.astype(F32)).astype(BF16)
    return hi, lo


def _split3(x):
    hi = x.astype(BF16)
    r = x - hi.astype(F32)
    mid = r.astype(BF16)
    lo = (r - mid.astype(F32)).astype(BF16)
    return hi, mid, lo


def _dot(a, b):
    return jnp.dot(a, b, preferred_element_type=F32)


def _dot_nt(a, b):
    return lax.dot_general(a, b, (((1,), (1,)), ((), ())), preferred_element_type=F32)


def _group_mean_sq(x, bd, group):
    hi, lo = _split2(x * x)
    return (_dot(hi, bd) + _dot(lo, bd)) * (1.0 / group)


def _block_diag_ones(n, group):
    idx = np.arange(n) // group
    return jnp.asarray((idx[:, None] == idx[None, :]).astype(np.float32), dtype=BF16)


def _even_in_kernel(x_ref, g_ref, w_ref, vg_ref, bd_ref, o_ref):
    h = _rms(x_ref[...], g_ref[...])
    p = _dot(h.astype(BF16), w_ref[...])
    v = p[:, A_WIDTH:2 * A_WIDTH]
    ms = _group_mean_sq(v, bd_ref[...], CHUNK)
    o_ref[:, 0:A_WIDTH] = p[:, 0:A_WIDTH]
    o_ref[:, A_WIDTH:2 * A_WIDTH] = v * lax.rsqrt(ms + EPS) * vg_ref[...]
    o_ref[:, 2 * A_WIDTH:] = p[:, 2 * A_WIDTH:]


def _even_in(x, g, w, vg, tm):
    n = x.shape[0]
    nout = w.shape[1]
    bd = _block_diag_ones(A_WIDTH, A_WIDTH // A_GROUPS)
    return pl.pallas_call(
        _even_in_kernel,
        grid=(n // tm,),
        in_specs=[
            pl.BlockSpec((tm, D_MODEL), lambda i: (i, 0)),
            pl.BlockSpec((1, D_MODEL), lambda i: (0, 0)),
            pl.BlockSpec((D_MODEL, nout), lambda i: (0, 0)),
            pl.BlockSpec((1, A_WIDTH), lambda i: (0, 0)),
            pl.BlockSpec((A_WIDTH, A_WIDTH), lambda i: (0, 0)),
        ],
        out_specs=pl.BlockSpec((tm, nout), lambda i: (i, 0)),
        out_shape=jax.ShapeDtypeStruct((n, nout), F32),
        compiler_params=_cparams(("parallel",)),
        name="even_in",
    )(x, g, w, vg, bd)


def _even_mix_kernel(p_ref, ctx_ref, x_ref, invc_ref, wmix_ref, mask_ref, bcol_ref, band_ref,
                     wpool_ref, pscale_ref, wout_ref, gffn_ref, x1_ref, h2_ref,
                     *, tile, pctx, seq_tiles):
    ctx = ctx_ref[...]
    if seq_tiles:
        first = (pl.program_id(0) % seq_tiles) == 0
        ctx = jnp.where(first, 0.0, ctx)
    z = p_ref[:, 2 * A_WIDTH:]
    zext = jnp.concatenate([ctx, z], axis=0)
    zhi, zlo = _split2(zext)
    mask = mask_ref[...]
    wms = [(wmix_ref[g] * mask).astype(BF16) for g in range(A_GROUPS)]
    rows = []
    for c in range(tile // CHUNK):
        r0 = c * CHUNK
        parts_a, parts_b = [], []
        for g in range(A_GROUPS):
            l0 = g * LANES
            u = p_ref[r0:r0 + CHUNK, l0:l0 + LANES]
            v = p_ref[r0:r0 + CHUNK, A_WIDTH + l0:A_WIDTH + l0 + LANES]
            s = _dot(wms[g], v.astype(BF16)) + bcol_ref[:, g:g + 1]
            parts_a.append((u * s).astype(BF16))
            band = band_ref[g]
            zh = zhi[r0:r0 + pctx + CHUNK, l0:l0 + LANES]
            zl = zlo[r0:r0 + pctx + CHUNK, l0:l0 + LANES]
            pooled = ((_dot(band, zh) + _dot(band, zl)) * invc_ref[r0:r0 + CHUNK, g:g + 1]
                      - z[r0:r0 + CHUNK, l0:l0 + LANES])
            yb = _dot(pooled.astype(BF16), wpool_ref[g]) * pscale_ref[:, l0:l0 + LANES]
            parts_b.append(yb.astype(BF16))
        rows.append(jnp.concatenate(parts_a + parts_b, axis=1))
    ab = jnp.concatenate(rows, axis=0) if len(rows) > 1 else rows[0]
    x1 = x_ref[...] + _dot(ab, wout_ref[...])
    x1_ref[...] = x1
    h2_ref[...] = _rms(x1, gffn_ref[...]).astype(BF16)


def _even_mix(p, ctx_arr, ctx_spec, x, invc, wmix, mask, bcol, band, wpool, pscale, wout, gffn,
              *, tile, pctx, seq_tiles):
    n = x.shape[0]
    kb = band.shape[2]
    const2 = lambda i: (0, 0)
    const3 = lambda i: (0, 0, 0)
    return pl.pallas_call(
        functools.partial(_even_mix_kernel, tile=tile, pctx=pctx, seq_tiles=seq_tiles),
        grid=(n // tile,),
        in_specs=[
            pl.BlockSpec((tile, 3 * A_WIDTH), lambda i: (i, 0)),
            ctx_spec,
            pl.BlockSpec((tile, D_MODEL), lambda i: (i, 0)),
            pl.BlockSpec((tile, A_GROUPS), lambda i: (i, 0)),
            pl.BlockSpec((A_GROUPS, CHUNK, CHUNK), const3),
            pl.BlockSpec((CHUNK, CHUNK), const2),
            pl.BlockSpec((CHUNK, A_GROUPS), const2),
            pl.BlockSpec((A_GROUPS, CHUNK, kb), const3),
            pl.BlockSpec((A_GROUPS, LANES, LANES), const3),
            pl.BlockSpec((1, B_WIDTH), const2),
            pl.BlockSpec((D_MODEL, D_MODEL), const2),
            pl.BlockSpec((1, D_MODEL), const2),
        ],
        out_specs=[pl.BlockSpec((tile, D_MODEL), lambda i: (i, 0)),
                   pl.BlockSpec((tile, D_MODEL), lambda i: (i, 0))],
        out_shape=[jax.ShapeDtypeStruct((n, D_MODEL), F32),
                   jax.ShapeDtypeStruct((n, D_MODEL), BF16)],
        compiler_params=_cparams(("parallel",)),
        name="even_mix",
    )(p, ctx_arr, x, invc, wmix, mask, bcol, band, wpool, pscale, wout, gffn)


def _prompt_even_consts():
    r = np.arange(CHUNK)
    mask = (r[None, :] <= r[:, None]).astype(np.float32)
    j = np.arange(2 * CHUNK)
    band = np.stack([((j[None, :] <= CHUNK + r[:, None]) & (j[None, :] > CHUNK + r[:, None] - w))
                     for w in POOL_WINDOWS]).astype(np.float32)
    pos = np.arange(SEQ)
    invc = np.stack([1.0 / np.minimum(pos + 1, w) for w in POOL_WINDOWS], axis=1).astype(np.float32)
    return jnp.asarray(mask), jnp.asarray(band, dtype=BF16), jnp.asarray(invc)


def _sample_even_consts():
    ns = DEC_BATCH * DEC_SEQ
    r = np.arange(ns)
    rb, rt = r // DEC_SEQ, r % DEC_SEQ
    mask = ((rb[:, None] == rb[None, :]) & (rt[None, :] <= rt[:, None])).astype(np.float32)
    nst = DEC_BATCH * (POOL_BUF + 1)
    j = np.arange(nst + ns)
    jb = np.where(j < nst, j // (POOL_BUF + 1), (j - nst) // DEC_SEQ)
    je = np.where(j < nst, j % (POOL_BUF + 1) - 1, POOL_BUF + (j - nst) % DEC_SEQ)
    re = POOL_BUF + rt
    band = np.stack([((jb[None, :] == rb[:, None]) & (je[None, :] <= re[:, None])
                      & (je[None, :] > re[:, None] - w)) for w in POOL_WINDOWS]).astype(np.float32)
    pos = PAST_LEN + rt
    invc = np.stack([1.0 / np.minimum(pos + 1, w) for w in POOL_WINDOWS], axis=1).astype(np.float32)
    return jnp.asarray(mask), jnp.asarray(band, dtype=BF16), jnp.asarray(invc)


def _swiglu_kernel(h_ref, x_ref, wg_ref, wu_ref, wd_ref, o_ref, *, nsplit):
    h = h_ref[...]
    ff = wg_ref.shape[1]
    step = ff // nsplit
    acc = x_ref[...]
    for s in range(nsplit):
        g = _dot(h, wg_ref[:, s * step:(s + 1) * step])
        u = _dot(h, wu_ref[:, s * step:(s + 1) * step])
        a = (g * jax.nn.sigmoid(g) * u).astype(BF16)
        acc = acc + _dot(a, wd_ref[s * step:(s + 1) * step, :])
    o_ref[...] = acc


def _swiglu(h, x, wg, wu, wd, tm):
    n = x.shape[0]
    ff = wg.shape[1]
    single = pl.Buffered(1)
    return pl.pallas_call(
        functools.partial(_swiglu_kernel, nsplit=2),
        grid=(n // tm,),
        in_specs=[
            pl.BlockSpec((tm, D_MODEL), lambda i: (i, 0)),
            pl.BlockSpec((tm, D_MODEL), lambda i: (i, 0)),
            pl.BlockSpec((D_MODEL, ff), lambda i: (0, 0), pipeline_mode=single),
            pl.BlockSpec((D_MODEL, ff), lambda i: (0, 0), pipeline_mode=single),
            pl.BlockSpec((ff, D_MODEL), lambda i: (0, 0), pipeline_mode=single),
        ],
        out_specs=pl.BlockSpec((tm, D_MODEL), lambda i: (i, 0)),
        out_shape=jax.ShapeDtypeStruct((n, D_MODEL), F32),
        compiler_params=_cparams(("parallel",)),
        name="swiglu_dense",
    )(h, x, wg, wu, wd)


def _odd_in_kernel(x_ref, g_ref, w_ref, qg_ref, kg_ref, bd_ref,
                   q_ref, k_ref, v_ref, kb_ref, vb_ref, gb_ref, zc_ref):
    h = _rms(x_ref[...], g_ref[...])
    p = _dot(h.astype(BF16), w_ref[...])
    bd = bd_ref[...]
    q = p[:, 0:C_WIDTH]
    k = p[:, C_WIDTH:2 * C_WIDTH]
    v = p[:, 2 * C_WIDTH:3 * C_WIDTH]
    qn = q * lax.rsqrt(_group_mean_sq(q, bd, C_HD) + EPS) * qg_ref[...]
    kn = k * lax.rsqrt(_group_mean_sq(k, bd, C_HD) + EPS) * kg_ref[...]
    q_ref[...] = (qn * (C_HD ** -0.5)).astype(BF16)
    k_ref[...] = kn
    kb_ref[...] = kn.astype(BF16)
    v_ref[...] = v
    vb_ref[...] = v.astype(BF16)
    gb_ref[...] = p[:, 3 * C_WIDTH:3 * C_WIDTH + D_WIDTH]
    zc_ref[...] = (p[:, 3 * C_WIDTH + D_WIDTH:3 * C_WIDTH + 2 * D_WIDTH]
                   * p[:, 3 * C_WIDTH + 2 * D_WIDTH:])


def _odd_in(x, g, w, qg, kg, tm):
    n = x.shape[0]
    nout = w.shape[1]
    bd = _block_diag_ones(C_WIDTH, C_HD)
    row = lambda i: (i, 0)
    const = lambda i: (0, 0)
    o512 = pl.BlockSpec((tm, C_WIDTH), row)
    return pl.pallas_call(
        _odd_in_kernel,
        grid=(n // tm,),
        in_specs=[
            pl.BlockSpec((tm, D_MODEL), row),
            pl.BlockSpec((1, D_MODEL), const),
            pl.BlockSpec((D_MODEL, nout), const),
            pl.BlockSpec((1, C_WIDTH), const),
            pl.BlockSpec((1, C_WIDTH), const),
            pl.BlockSpec((C_WIDTH, C_WIDTH), const),
        ],
        out_specs=[o512] * 7,
        out_shape=[jax.ShapeDtypeStruct((n, C_WIDTH), BF16),
                   jax.ShapeDtypeStruct((n, C_WIDTH), F32),
                   jax.ShapeDtypeStruct((n, C_WIDTH), F32),
                   jax.ShapeDtypeStruct((n, C_WIDTH), BF16),
                   jax.ShapeDtypeStruct((n, C_WIDTH), BF16),
                   jax.ShapeDtypeStruct((n, D_WIDTH), F32),
                   jax.ShapeDtypeStruct((n, D_WIDTH), F32)],
        compiler_params=_cparams(("parallel",)),
        name="odd_in",
    )(x, g, w, qg, kg, bd)


def _sb_block(z, carry, upper, valid):
    l = jnp.log(1.0 + jnp.exp(-jnp.abs(z)))
    log_beta = jnp.minimum(z, 0.0) - l
    log_rem = log_beta - z
    if valid is not None:
        log_rem = jnp.where(valid, log_rem, 0.0)
    hi, lo = _split2(log_rem)
    later = _dot(hi, upper) + _dot(lo, upper)
    w = jnp.exp(log_beta + later + carry)
    if valid is not None:
        w = jnp.where(valid, w, 0.0)
    return w, carry + jnp.sum(log_rem, axis=1, keepdims=True)


def _upper_ones(n):
    return (lax.broadcasted_iota(jnp.int32, (n, n), 0)
            > lax.broadcasted_iota(jnp.int32, (n, n), 1)).astype(F32).astype(BF16)


def _attn_prompt_kernel(bias_ref, q_ref, k_ref, v_ref, o_ref, *, tq):
    hp = pl.program_id(1)
    i = pl.program_id(2)
    qf = q_ref[...].astype(F32)
    lane_head = lax.broadcasted_iota(jnp.int32, (tq, LANES), 1) // C_HD
    causal = (lax.broadcasted_iota(jnp.int32, (tq, tq), 1)
              < lax.broadcasted_iota(jnp.int32, (tq, tq), 0))
    upper = _upper_ones(tq)
    outs = []
    for hh in range(2):
        qh = jnp.where(lane_head == hh, qf, 0.0).astype(BF16)
        bias = bias_ref[hp * 2 + hh]

        def block(j, carry, acc, valid, qh=qh, bias=bias):
            start = pl.multiple_of(j * tq, tq)
            kb = k_ref[pl.ds(start, tq), :]
            vb = v_ref[pl.ds(start, tq), :]
            z = _dot_nt(qh, kb) + bias
            w, carry = _sb_block(z, carry, upper, valid)
            return carry, acc + _dot(w.astype(BF16), vb)

        carry, acc = block(i, jnp.zeros((tq, 1), F32), jnp.zeros((tq, LANES), F32), causal)

        def body(t, ca, block=block):
            return block(i - 1 - t, ca[0], ca[1], None)

        carry, acc = lax.fori_loop(0, i, body, (carry, acc))
        outs.append(acc)
    o_ref[...] = jnp.where(lane_head == 0, outs[0], outs[1]).astype(BF16)


def _attn_prompt(q, kb, vb, bias, batch, seq):
    n = q.shape[0]
    tq = ATT_TQ
    nq = seq // tq
    return pl.pallas_call(
        functools.partial(_attn_prompt_kernel, tq=tq),
        grid_spec=pltpu.PrefetchScalarGridSpec(
            num_scalar_prefetch=1,
            grid=(batch, C_HEADS // 2, nq),
            in_specs=[
                pl.BlockSpec((tq, LANES), lambda b, hp, i, bias: (b * nq + i, hp)),
                pl.BlockSpec((seq, LANES), lambda b, hp, i, bias: (b, hp)),
                pl.BlockSpec((seq, LANES), lambda b, hp, i, bias: (b, hp)),
            ],
            out_specs=pl.BlockSpec((tq, LANES), lambda b, hp, i, bias: (b * nq + i, hp)),
        ),
        out_shape=jax.ShapeDtypeStruct((n, C_WIDTH), BF16),
        compiler_params=_cparams(("parallel", "parallel", "arbitrary")),
        name="attn_prompt",
    )(bias, q, kb, vb)


def _attn_sample_kernel(pt_ref, q_ref, kn_ref, vn_ref, bias_ref, *rest, layer, nsteps):
    del pt_ref, layer
    g_pages = PAGES_PER_STEP
    k_refs = rest[:g_pages]
    v_refs = rest[g_pages:2 * g_pages]
    o_ref, acc_ref, carry_ref, qb_ref = rest[2 * g_pages:]
    j = pl.program_id(1)
    nrow = DEC_SEQ * C_HEADS
    row = lax.broadcasted_iota(jnp.int32, (nrow, C_WIDTH), 0)
    head_lane = (lax.broadcasted_iota(jnp.int32, (nrow, C_WIDTH), 1) // C_HD) == (row % C_HEADS)
    upper = _upper_ones(PAGE_SIZE)
    bias = bias_ref[...]

    @pl.when(j == 0)
    def _():
        qf = q_ref[...]
        qrep = jnp.concatenate(
            [jnp.broadcast_to(qf[t:t + 1, :], (C_HEADS, C_WIDTH)) for t in range(DEC_SEQ)], axis=0)
        qb = jnp.where(head_lane, qrep, 0.0).astype(BF16)
        qb_ref[...] = qb
        pad = jnp.zeros((PAGE_SIZE - 8, C_WIDTH), F32)
        kn = jnp.concatenate([kn_ref[...], pad], axis=0).astype(BF16)
        vn = jnp.concatenate([vn_ref[...], pad], axis=0).astype(BF16)
        z = _dot_nt(qb, kn) + bias
        kcol = lax.broadcasted_iota(jnp.int32, (nrow, PAGE_SIZE), 1)
        qrow = lax.broadcasted_iota(jnp.int32, (nrow, PAGE_SIZE), 0) // C_HEADS
        w, carry = _sb_block(z, jnp.zeros((nrow, 1), F32), upper, kcol < qrow)
        acc_ref[...] = _dot(w.astype(BF16), vn)
        carry_ref[...] = carry

    qb = qb_ref[...]
    acc = acc_ref[...]
    carry = carry_ref[...]
    for g in range(g_pages - 1, -1, -1):
        kp = k_refs[g][...].astype(BF16)
        vp = v_refs[g][...].astype(BF16)
        z = _dot_nt(qb, kp) + bias
        w, carry = _sb_block(z, carry, upper, None)
        acc = acc + _dot(w.astype(BF16), vp)
    acc_ref[...] = acc
    carry_ref[...] = carry

    @pl.when(j == nsteps - 1)
    def _():
        masked = jnp.where(head_lane, acc, 0.0)
        orow = lax.broadcasted_iota(jnp.int32, (8, C_WIDTH), 0)
        out = jnp.zeros((8, C_WIDTH), F32)
        for t in range(DEC_SEQ):
            tok = jnp.sum(masked[t * C_HEADS:(t + 1) * C_HEADS, :], axis=0, keepdims=True)
            out = jnp.where(orow == t, tok, out)
        o_ref[...] = out


def _attn_sample(q, kn, vn, bias_col, cache_k, cache_v, page_table, layer):
    g_pages = PAGES_PER_STEP
    nsteps = N_PAGES // g_pages
    nrow = DEC_SEQ * C_HEADS
    pad8 = lambda a: jnp.pad(a.astype(F32), ((0, 0), (0, 8 - DEC_SEQ), (0, 0)))
    q, kn, vn = pad8(q), pad8(kn), pad8(vn)

    def page_spec(g):
        def imap(b, j, pt):
            return (layer, pt[b * N_PAGES + (nsteps - 1 - j) * g_pages + g], 0, 0)
        return pl.BlockSpec((None, None, PAGE_SIZE, C_WIDTH), imap)

    seq3 = lambda b, j, pt: (b, 0, 0)
    out = pl.pallas_call(
        functools.partial(_attn_sample_kernel, layer=layer, nsteps=nsteps),
        grid_spec=pltpu.PrefetchScalarGridSpec(
            num_scalar_prefetch=1,
            grid=(DEC_BATCH, nsteps),
            in_specs=[pl.BlockSpec((None, 8, C_WIDTH), seq3),
                      pl.BlockSpec((None, 8, C_WIDTH), seq3),
                      pl.BlockSpec((None, 8, C_WIDTH), seq3),
                      pl.BlockSpec((nrow, 1), lambda b, j, pt: (0, 0))]
                     + [page_spec(g) for g in range(g_pages)]
                     + [page_spec(g) for g in range(g_pages)],
            out_specs=pl.BlockSpec((None, 8, C_WIDTH), seq3),
            scratch_shapes=[pltpu.VMEM((nrow, C_WIDTH), F32),
                            pltpu.VMEM((nrow, 1), F32),
                            pltpu.VMEM((nrow, C_WIDTH), BF16)],
        ),
        out_shape=jax.ShapeDtypeStruct((DEC_BATCH, 8, C_WIDTH), F32),
        compiler_params=_cparams(("parallel", "arbitrary")),
        name="attn_sample",
    )(page_table.reshape(-1), q, kn, vn, bias_col, *([cache_k] * g_pages), *([cache_v] * g_pages))
    return out[:, :DEC_SEQ].astype(BF16)


def _odd_tail(att, d, x, wout_ref, gffn_ref, rhi_ref, rlo_ref, x1_ref, h2_ref, route_ref):
    ad = jnp.concatenate([att, d.astype(BF16)], axis=1)
    x1 = x + _dot(ad, wout_ref[...])
    x1_ref[...] = x1
    h2 = _rms(x1, gffn_ref[...])
    h2_ref[...] = h2
    hhi, hlo = _split2(h2)
    rhi = rhi_ref[...]
    logits = _dot(hhi, rhi) + _dot(hlo, rhi) + _dot(hhi, rlo_ref[...])
    lane = lax.broadcasted_iota(jnp.int32, logits.shape, 1).astype(F32)
    neg = jnp.float32(-jnp.inf)
    l1 = jnp.where(lane < N_EXPERTS, logits, neg)
    v1 = jnp.max(l1, axis=1, keepdims=True)
    e1 = jnp.min(jnp.where(l1 == v1, lane, float(LANES)), axis=1, keepdims=True)
    l2 = jnp.where(lane == e1, neg, l1)
    v2 = jnp.max(l2, axis=1, keepdims=True)
    e2 = jnp.min(jnp.where(l2 == v2, lane, float(LANES)), axis=1, keepdims=True)
    ex = jnp.exp(v2 - v1)
    g1 = 1.0 / (1.0 + ex)
    g2 = ex / (1.0 + ex)
    route_ref[...] = jnp.where(
        lane == 0, e1, jnp.where(lane == 1, e2, jnp.where(lane == 2, g1, jnp.where(lane == 3, g2, 0.0))))


def _odd_mix_prompt_kernel(att_ref, gb_ref, zc_ref, halo_ref, x_ref, cw_ref, wout_ref, gffn_ref,
                           rhi_ref, rlo_ref, x1_ref, h2_ref, route_ref, *, tile, seq_tiles):
    halo = halo_ref[...]
    first = (pl.program_id(0) % seq_tiles) == 0
    halo = jnp.where(first, 0.0, halo)
    zc = zc_ref[...]
    ext = jnp.concatenate([halo, zc], axis=0)
    hs = halo.shape[0]
    z1 = pltpu.roll(ext, 1, axis=0)[hs:, :]
    z2 = pltpu.roll(ext, 2, axis=0)[hs:, :]
    conv = z2 * cw_ref[0:1, :] + z1 * cw_ref[1:2, :] + zc * cw_ref[2:3, :]
    d = gb_ref[...] * conv
    _odd_tail(att_ref[...], d, x_ref[...], wout_ref, gffn_ref, rhi_ref, rlo_ref,
              x1_ref, h2_ref, route_ref)


def _odd_mix_sample_kernel(att_ref, gb_ref, ext_ref, sel_ref, x_ref, cw_ref, wout_ref, gffn_ref,
                           rhi_ref, rlo_ref, x1_ref, h2_ref, route_ref):
    parts = _split3(ext_ref[...])
    conv = None
    for jj in range(CONV_W):
        sel = sel_ref[jj]
        shifted = _dot(sel, parts[0]) + _dot(sel, parts[1]) + _dot(sel, parts[2])
        term = shifted * cw_ref[jj:jj + 1, :]
        conv = term if conv is None else conv + term
    d = gb_ref[...] * conv
    _odd_tail(att_ref[...], d, x_ref[...], wout_ref, gffn_ref, rhi_ref, rlo_ref,
              x1_ref, h2_ref, route_ref)


def _odd_mix_out(n, tile):
    row = lambda i: (i, 0)
    specs = [pl.BlockSpec((tile, D_MODEL), row), pl.BlockSpec((tile, D_MODEL), row),
             pl.BlockSpec((tile, LANES), row)]
    shapes = [jax.ShapeDtypeStruct((n, D_MODEL), F32), jax.ShapeDtypeStruct((n, D_MODEL), F32),
              jax.ShapeDtypeStruct((n, LANES), F32)]
    return specs, shapes


def _odd_mix_prompt(att, gb, zc, x, cw, wout, gffn, rhi, rlo, tile, seq):
    n = x.shape[0]
    row = lambda i: (i, 0)
    const = lambda i: (0, 0)
    hb = tile // 8
    out_specs, out_shape = _odd_mix_out(n, tile)
    return pl.pallas_call(
        functools.partial(_odd_mix_prompt_kernel, tile=tile, seq_tiles=seq // tile),
        grid=(n // tile,),
        in_specs=[
            pl.BlockSpec((tile, C_WIDTH), row),
            pl.BlockSpec((tile, D_WIDTH), row),
            pl.BlockSpec((tile, D_WIDTH), row),
            pl.BlockSpec((8, D_WIDTH), lambda i: (jnp.maximum(i * hb - 1, 0), 0)),
            pl.BlockSpec((tile, D_MODEL), row),
            pl.BlockSpec((CONV_W, D_WIDTH), const),
            pl.BlockSpec((D_MODEL, D_MODEL), const),
            pl.BlockSpec((1, D_MODEL), const),
            pl.BlockSpec((D_MODEL, LANES), const),
            pl.BlockSpec((D_MODEL, LANES), const),
        ],
        out_specs=out_specs,
        out_shape=out_shape,
        compiler_params=_cparams(("parallel",)),
        name="odd_mix_prompt",
    )(att, gb, zc, zc, x, cw, wout, gffn, rhi, rlo)


def _odd_mix_sample(att, gb, ext, sel, x, cw, wout, gffn, rhi, rlo):
    n = x.shape[0]
    const = lambda i: (0, 0)
    out_specs, out_shape = _odd_mix_out(n, n)
    return pl.pallas_call(
        _odd_mix_sample_kernel,
        grid=(1,),
        in_specs=[
            pl.BlockSpec((n, C_WIDTH), const),
            pl.BlockSpec((n, D_WIDTH), const),
            pl.BlockSpec(ext.shape, const),
            pl.BlockSpec(sel.shape, lambda i: (0, 0, 0)),
            pl.BlockSpec((n, D_MODEL), const),
            pl.BlockSpec((CONV_W, D_WIDTH), const),
            pl.BlockSpec((D_MODEL, D_MODEL), const),
            pl.BlockSpec((1, D_MODEL), const),
            pl.BlockSpec((D_MODEL, LANES), const),
            pl.BlockSpec((D_MODEL, LANES), const),
        ],
        out_specs=out_specs,
        out_shape=out_shape,
        compiler_params=_cparams(("arbitrary",)),
        name="odd_mix_sample",
    )(att, gb, ext, sel, x, cw, wout, gffn, rhi, rlo)


def _sample_conv_select():
    ns = DEC_BATCH * DEC_SEQ
    r = np.arange(ns)
    rb, rt = r // DEC_SEQ, r % DEC_SEQ
    c = np.arange(DEC_BATCH * 8)
    sel = np.stack([(c[None, :] == (rb * 8 + rt + jj)[:, None]) for jj in range(CONV_W)])
    return jnp.asarray(sel.astype(np.float32), dtype=BF16)


def _row_copy(src_hbm, src_row, dst, dst_row, sem):
    return pltpu.make_async_copy(src_hbm.at[pl.ds(src_row, 1), :], dst.at[pl.ds(dst_row, 1), :], sem)


def _gather_rows(idx_ref, src_hbm, dst, sem, nrows):
    def start(r, c):
        _row_copy(src_hbm, idx_ref[0, 0, r], dst, r, sem).start()
        return c

    lax.fori_loop(0, nrows, start, 0)

    def wait(r, c):
        _row_copy(src_hbm, 0, dst, r, sem).wait()
        return c

    lax.fori_loop(0, nrows, wait, 0)


def _moe_gather_kernel(src_ref, h_hbm, o_ref, buf, sem, *, tm):
    _gather_rows(src_ref, h_hbm, buf, sem, tm)
    o_ref[...] = buf[...].astype(BF16)


def _moe_gather(src, h, tm):
    ntiles = src.shape[0]
    return pl.pallas_call(
        functools.partial(_moe_gather_kernel, tm=tm),
        grid=(ntiles,),
        in_specs=[pl.BlockSpec((1, 1, tm), lambda t: (t, 0, 0), memory_space=pltpu.SMEM),
                  pl.BlockSpec(memory_space=pl.ANY)],
        out_specs=pl.BlockSpec((tm, D_MODEL), lambda t: (t, 0)),
        out_shape=jax.ShapeDtypeStruct((ntiles * tm, D_MODEL), BF16),
        scratch_shapes=[pltpu.VMEM((tm, D_MODEL), F32), pltpu.SemaphoreType.DMA(())],
        compiler_params=_cparams(("arbitrary",)),
        name="moe_gather",
    )(src, h)


def _moe_ffn_kernel(te_ref, nused_ref, x_ref, wg_ref, wu_ref, wd_ref, o_ref, acc_ref, *, nf):
    del te_ref
    t = pl.program_id(0)
    f = pl.program_id(1)
    used = t < nused_ref[0]

    @pl.when(f == 0)
    def _():
        acc_ref[...] = jnp.zeros_like(acc_ref)

    @pl.when(used)
    def _():
        x = x_ref[...]
        g = _dot(x, wg_ref[...])
        u = _dot(x, wu_ref[...])
        a = (g * jax.nn.sigmoid(g) * u).astype(BF16)
        acc_ref[...] += _dot(a, wd_ref[...])

    @pl.when(f == nf - 1)
    def _():
        o_ref[...] = acc_ref[...]


def _moe_ffn(tile_expert, nused, xs, wg, wu, wd, tm, tf):
    p = xs.shape[0]
    ff = wg.shape[2]
    nf = ff // tf
    return pl.pallas_call(
        functools.partial(_moe_ffn_kernel, nf=nf),
        grid_spec=pltpu.PrefetchScalarGridSpec(
            num_scalar_prefetch=2,
            grid=(p // tm, nf),
            in_specs=[
                pl.BlockSpec((tm, D_MODEL), lambda t, f, te, nu: (t, 0)),
                pl.BlockSpec((None, D_MODEL, tf), lambda t, f, te, nu: (te[t], 0, f)),
                pl.BlockSpec((None, D_MODEL, tf), lambda t, f, te, nu: (te[t], 0, f)),
                pl.BlockSpec((None, tf, D_MODEL), lambda t, f, te, nu: (te[t], f, 0)),
            ],
            out_specs=pl.BlockSpec((tm, D_MODEL), lambda t, f, te, nu: (t, 0)),
            scratch_shapes=[pltpu.VMEM((tm, D_MODEL), F32)],
        ),
        out_shape=jax.ShapeDtypeStruct((p, D_MODEL), F32),
        compiler_params=_cparams(("parallel", "arbitrary")),
        name="moe_ffn",
    )(tile_expert, nused, xs, wg, wu, wd)


def _moe_combine_kernel(p0_ref, p1_ref, x_ref, route_ref, y_hbm, o_ref, b0, b1, sem0, sem1, *, tile):
    _gather_rows(p0_ref, y_hbm, b0, sem0, tile)
    _gather_rows(p1_ref, y_hbm, b1, sem1, tile)
    g1 = route_ref[:, 2:3]
    g2 = route_ref[:, 3:4]
    o_ref[...] = x_ref[...] + g1 * b0[...] + g2 * b1[...]


def _moe_combine(pos0, pos1, x, route, ys, tile):
    n = x.shape[0]
    row = lambda t: (t, 0)
    smem = lambda: pl.BlockSpec((1, 1, tile), lambda t: (t, 0, 0), memory_space=pltpu.SMEM)
    return pl.pallas_call(
        functools.partial(_moe_combine_kernel, tile=tile),
        grid=(n // tile,),
        in_specs=[smem(), smem(),
                  pl.BlockSpec((tile, D_MODEL), row),
                  pl.BlockSpec((tile, LANES), row),
                  pl.BlockSpec(memory_space=pl.ANY)],
        out_specs=pl.BlockSpec((tile, D_MODEL), row),
        out_shape=jax.ShapeDtypeStruct((n, D_MODEL), F32),
        scratch_shapes=[pltpu.VMEM((tile, D_MODEL), F32), pltpu.VMEM((tile, D_MODEL), F32),
                        pltpu.SemaphoreType.DMA(()), pltpu.SemaphoreType.DMA(())],
        compiler_params=_cparams(("arbitrary",)),
        name="moe_combine",
    )(pos0, pos1, x, route, ys)


def _moe(h2, x1, route, wg, wu, wd):
    n = h2.shape[0]
    tm = MOE_TM
    ntiles = (TOP_K * n + N_EXPERTS * (tm - 1)) // tm + 1
    experts = route[:, 0:TOP_K].astype(jnp.int32)
    flat = experts.reshape(-1)
    onehot = (flat[:, None] == jnp.arange(N_EXPERTS, dtype=jnp.int32)[None, :]).astype(jnp.int32)
    csum = jnp.cumsum(onehot, axis=0)
    counts = csum[-1]
    rank = jnp.sum((csum - onehot) * onehot, axis=1)
    padded = ((counts + tm - 1) // tm) * tm
    ends = jnp.cumsum(padded)
    offs = ends - padded
    dest = offs[flat] + rank
    token = jnp.arange(TOP_K * n, dtype=jnp.int32) // TOP_K
    src = jnp.zeros((ntiles * tm,), jnp.int32).at[dest].set(token)
    tile_start = jnp.arange(ntiles, dtype=jnp.int32) * tm
    tile_expert = jnp.minimum(jnp.searchsorted(ends, tile_start, side="right"),
                              N_EXPERTS - 1).astype(jnp.int32)
    nused = (ends[-1] // tm).astype(jnp.int32).reshape(1)
    xs = _moe_gather(src.reshape(ntiles, 1, tm), h2, tm)
    ys = _moe_ffn(tile_expert, nused, xs, wg, wu, wd, tm, MOE_TF)
    ctile = 128
    pos = dest.reshape(n, TOP_K)
    pos0 = pos[:, 0].reshape(n // ctile, 1, ctile)
    pos1 = pos[:, 1].reshape(n // ctile, 1, ctile)
    return _moe_combine(pos0, pos1, x1, route, ys, ctile)


def kernel(x_prompt, x_sample, cache_k, cache_v, page_table, state_pool, state_conv, norm_mix, norm_ffn,
           w_in_even, w_out_even, v_norm_g, w_spatial, b_spatial, w_pool, pool_scale,
           w_in_odd, w_out_odd, q_norm_g, k_norm_g, sb_bias, conv_w,
           w_gate_dense, w_up_dense, w_down_dense, w_router, w_gate_exp, w_up_exp, w_down_exp):
    bp, seq, _ = x_prompt.shape
    bs, ts, _ = x_sample.shape
    np_rows = bp * seq
    ns_rows = bs * ts
    xp = x_prompt.reshape(np_rows, D_MODEL)
    xs = x_sample.reshape(ns_rows, D_MODEL)
    n_pool = cache_k.shape[1]
    ck = cache_k.reshape(cache_k.shape[0], n_pool, PAGE_SIZE, C_WIDTH)
    cv = cache_v.reshape(cache_v.shape[0], n_pool, PAGE_SIZE, C_WIDTH)

    mask_p, band_p, invc_p = _prompt_even_consts()
    invc_p = jnp.tile(invc_p, (bp, 1))
    mask_s, band_s, invc_s = _sample_even_consts()
    sel_s = _sample_conv_select()
    tile = ROW_TILE

    kp_l, vp_l, ks_l, vs_l = [], [], [], []
    poolp_l, pools_l, convp_l, convs_l, chunkv_l = [], [], [], [], []
    for layer in range(DEPTH):
        i = layer // 2
        g_mix = norm_mix[layer].reshape(1, D_MODEL)
        g_ffn = norm_ffn[layer].reshape(1, D_MODEL)
        if layer % 2 == 0:
            w_in = w_in_even[i].astype(BF16)
            w_out = w_out_even[i].astype(BF16)
            vg = v_norm_g[i].reshape(1, A_WIDTH)
            wpool = w_pool[i].astype(BF16)
            pscale = pool_scale[i].reshape(1, B_WIDTH)
            pp = _even_in(xp, g_mix, w_in, vg, tile)
            ctx_spec = pl.BlockSpec(
                (CHUNK, B_WIDTH), lambda t: (jnp.maximum(t * (tile // CHUNK) - 1, 0), 2))
            xp1, hp2 = _even_mix(pp, pp, ctx_spec, xp, invc_p, w_spatial[i], mask_p,
                                 b_spatial[i].T, band_p, wpool, pscale, w_out, g_ffn,
                                 tile=tile, pctx=CHUNK, seq_tiles=seq // tile)
            poolp_l.append(pp.reshape(bp, seq, 3 * A_WIDTH)[:, seq - POOL_BUF:, 2 * A_WIDTH:])
            ps = _even_in(xs, g_mix, w_in, vg, ns_rows)
            state = jnp.pad(state_pool[i], ((0, 0), (1, 0), (0, 0))).reshape(bs * (POOL_BUF + 1), B_WIDTH)
            wmix_s = jnp.tile(w_spatial[i][:, :ts, :ts], (1, bs, bs))
            bcol_s = jnp.tile(b_spatial[i][:, :ts], (1, bs)).T
            ctx_spec_s = pl.BlockSpec(state.shape, lambda t: (0, 0))
            xs1, hs2 = _even_mix(ps, state, ctx_spec_s, xs, invc_s, wmix_s, mask_s, bcol_s, band_s,
                                 wpool, pscale, w_out, g_ffn,
                                 tile=ns_rows, pctx=state.shape[0], seq_tiles=0)
            z_s = ps[:, 2 * A_WIDTH:].reshape(bs, ts, B_WIDTH)
            pools_l.append(jnp.concatenate([state_pool[i], z_s], axis=1)[:, -POOL_BUF:])
            chunkv_l.append(ps[:, A_WIDTH:2 * A_WIDTH].reshape(bs, ts, A_WIDTH))
            wg = w_gate_dense[i].astype(BF16)
            wu = w_up_dense[i].astype(BF16)
            wd = w_down_dense[i].astype(BF16)
            xp = _swiglu(hp2, xp1, wg, wu, wd, tile)
            xs = _swiglu(hs2, xs1, wg, wu, wd, ns_rows)
        else:
            w_in = w_in_odd[i].astype(BF16)
            w_out = w_out_odd[i].astype(BF16)
            qg = jnp.tile(q_norm_g[i], C_HEADS).reshape(1, C_WIDTH)
            kg = jnp.tile(k_norm_g[i], C_HEADS).reshape(1, C_WIDTH)
            router = jnp.pad(w_router[i], ((0, 0), (0, LANES - N_EXPERTS)))
            rhi = router.astype(BF16)
            rlo = (router - rhi.astype(F32)).astype(BF16)
            cw = conv_w[i]
            q_p, k_p, v_p, kb_p, vb_p, gb_p, zc_p = _odd_in(xp, g_mix, w_in, qg, kg, tile)
            att_p = _attn_prompt(q_p, kb_p, vb_p, sb_bias[i], bp, seq)
            xp1, hp2, route_p = _odd_mix_prompt(att_p, gb_p, zc_p, xp, cw, w_out, g_ffn, rhi, rlo,
                                                tile, seq)
            kp_l.append(k_p.reshape(bp, seq, C_HEADS, C_HD))
            vp_l.append(v_p.reshape(bp, seq, C_HEADS, C_HD))
            convp_l.append(zc_p.reshape(bp, seq, D_WIDTH)[:, seq - (CONV_W - 1):])
            q_s, k_s, v_s, _, _, gb_s, zc_s = _odd_in(xs, g_mix, w_in, qg, kg, ns_rows)
            bias_col = jnp.tile(sb_bias[i], ts).reshape(ts * C_HEADS, 1)
            att_s = _attn_sample(q_s.reshape(bs, ts, C_WIDTH), k_s.reshape(bs, ts, C_WIDTH),
                                 v_s.reshape(bs, ts, C_WIDTH), bias_col, ck, cv, page_table, i)
            zc_s3 = zc_s.reshape(bs, ts, D_WIDTH)
            ext = jnp.concatenate(
                [state_conv[i], zc_s3, jnp.zeros((bs, 8 - (CONV_W - 1) - ts, D_WIDTH), F32)], axis=1)
            xs1, hs2, route_s = _odd_mix_sample(att_s.reshape(ns_rows, C_WIDTH), gb_s,
                                                ext.reshape(bs * 8, D_WIDTH), sel_s, xs, cw, w_out,
                                                g_ffn, rhi, rlo)
            ks_l.append(k_s.reshape(bs, ts, C_HEADS, C_HD))
            vs_l.append(v_s.reshape(bs, ts, C_HEADS, C_HD))
            convs_l.append(zc_s3[:, ts - (CONV_W - 1):])
            h2 = jnp.concatenate([hp2, hs2], axis=0)
            x1 = jnp.concatenate([xp1, xs1], axis=0)
            route = jnp.concatenate([route_p, route_s], axis=0)
            x2 = _moe(h2, x1, route, w_gate_exp[i].astype(BF16), w_up_exp[i].astype(BF16),
                      w_down_exp[i].astype(BF16))
            xp = x2[:np_rows]
            xs = x2[np_rows:]
    return (xp.reshape(bp, seq, D_MODEL), xs.reshape(bs, ts, D_MODEL),
            jnp.stack(kp_l), jnp.stack(vp_l), jnp.stack(ks_l), jnp.stack(vs_l),
            jnp.stack(poolp_l), jnp.stack(pools_l), jnp.stack(convp_l), jnp.stack(convs_l),
            jnp.stack(chunkv_l))
```

```python
import functools

import numpy as np
import jax
import jax.numpy as jnp
from jax import lax
from jax.experimental import pallas as pl
from jax.experimental.pallas import tpu as pltpu

F32 = jnp.float32
BF16 = jnp.bfloat16

D_MODEL = 1024
SEQ = 8192
DEPTH = 4
DEC_BATCH = 32
DEC_SEQ = 4
PAST_LEN = 8192
PAGE_SIZE = 128
N_PAGES = PAST_LEN // PAGE_SIZE
A_GROUPS = 4
A_WIDTH = 512
CHUNK = 128
POOL_WINDOWS = (2, 4, 8, 16)
B_WIDTH = 512
POOL_BUF = 15
C_HEADS = 8
C_HD = 64
C_WIDTH = 512
D_WIDTH = 512
CONV_W = 3
FF_DENSE = 2816
N_EXPERTS = 8
TOP_K = 2
FF_EXPERT = 3584
EPS = 1e-6

LANES = 128
ROW_TILE = 512
ATT_TQ = 256
LOG2E = 1.4426950408889634
PAGES_PER_STEP = 8
MOE_TM = 1024
MOE_GATHER_ROWS = 512
MOE_TF = 512
VMEM_LIMIT = 56 * 1024 * 1024


def _cparams(sem):
    return pltpu.CompilerParams(dimension_semantics=sem, vmem_limit_bytes=VMEM_LIMIT)


def _rms(x, g):
    return x * lax.rsqrt(jnp.mean(x * x, axis=-1, keepdims=True) + EPS) * g


def _split2(x):
    hi = x.astype(BF16)
    lo = (x - hi.astype(F32)).astype(BF16)
    return hi, lo


def _split3(x):
    hi = x.astype(BF16)
    r = x - hi.astype(F32)
    mid = r.astype(BF16)
    lo = (r - mid.astype(F32)).astype(BF16)
    return hi, mid, lo


def _dot(a, b):
    return jnp.dot(a, b, preferred_element_type=F32)


def _dot_nt(a, b):
    return lax.dot_general(a, b, (((1,), (1,)), ((), ())), preferred_element_type=F32)


def _group_mean_sq(x, bd, group):
    hi, lo = _split2(x * x)
    return (_dot(hi, bd) + _dot(lo, bd)) * (1.0 / group)


def _block_diag_ones(n, group):
    idx = np.arange(n) // group
    return jnp.asarray((idx[:, None] == idx[None, :]).astype(np.float32), dtype=BF16)


def _even_in_kernel(x_ref, g_ref, w_ref, vg_ref, bd_ref, o_ref):
    h = _rms(x_ref[...], g_ref[...])
    p = _dot(h.astype(BF16), w_ref[...])
    v = p[:, A_WIDTH:2 * A_WIDTH]
    ms = _group_mean_sq(v, bd_ref[...], CHUNK)
    o_ref[:, 0:A_WIDTH] = p[:, 0:A_WIDTH]
    o_ref[:, A_WIDTH:2 * A_WIDTH] = v * lax.rsqrt(ms + EPS) * vg_ref[...]
    o_ref[:, 2 * A_WIDTH:] = p[:, 2 * A_WIDTH:]


def _even_in(x, g, w, vg, tm):
    n = x.shape[0]
    nout = w.shape[1]
    bd = _block_diag_ones(A_WIDTH, A_WIDTH // A_GROUPS)
    return pl.pallas_call(
        _even_in_kernel,
        grid=(n // tm,),
        in_specs=[
            pl.BlockSpec((tm, D_MODEL), lambda i: (i, 0)),
            pl.BlockSpec((1, D_MODEL), lambda i: (0, 0)),
            pl.BlockSpec((D_MODEL, nout), lambda i: (0, 0)),
            pl.BlockSpec((1, A_WIDTH), lambda i: (0, 0)),
            pl.BlockSpec((A_WIDTH, A_WIDTH), lambda i: (0, 0)),
        ],
        out_specs=pl.BlockSpec((tm, nout), lambda i: (i, 0)),
        out_shape=jax.ShapeDtypeStruct((n, nout), F32),
        compiler_params=_cparams(("parallel",)),
        name="even_in",
    )(x, g, w, vg, bd)


def _even_mix_kernel(p_ref, ctx_ref, x_ref, invc_ref, wmix_ref, mask_ref, bcol_ref, band_ref,
                     wpool_ref, pscale_ref, wout_ref, gffn_ref, x1_ref, h2_ref,
                     *, tile, pctx, seq_tiles):
    ctx = ctx_ref[...]
    if seq_tiles:
        first = (pl.program_id(0) % seq_tiles) == 0
        ctx = jnp.where(first, 0.0, ctx)
    z = p_ref[:, 2 * A_WIDTH:]
    zext = jnp.concatenate([ctx, z], axis=0)
    zhi, zlo = _split2(zext)
    mask = mask_ref[...]
    wms = [(wmix_ref[g] * mask).astype(BF16) for g in range(A_GROUPS)]
    rows = []
    for c in range(tile // CHUNK):
        r0 = c * CHUNK
        parts_a, parts_b = [], []
        for g in range(A_GROUPS):
            l0 = g * LANES
            u = p_ref[r0:r0 + CHUNK, l0:l0 + LANES]
            v = p_ref[r0:r0 + CHUNK, A_WIDTH + l0:A_WIDTH + l0 + LANES]
            s = _dot(wms[g], v.astype(BF16)) + bcol_ref[:, g:g + 1]
            parts_a.append((u * s).astype(BF16))
            band = band_ref[g]
            zh = zhi[r0:r0 + pctx + CHUNK, l0:l0 + LANES]
            zl = zlo[r0:r0 + pctx + CHUNK, l0:l0 + LANES]
            pooled = ((_dot(band, zh) + _dot(band, zl)) * invc_ref[r0:r0 + CHUNK, g:g + 1]
                      - z[r0:r0 + CHUNK, l0:l0 + LANES])
            yb = _dot(pooled.astype(BF16), wpool_ref[g]) * pscale_ref[:, l0:l0 + LANES]
            parts_b.append(yb.astype(BF16))
        rows.append(jnp.concatenate(parts_a + parts_b, axis=1))
    ab = jnp.concatenate(rows, axis=0) if len(rows) > 1 else rows[0]
    x1 = x_ref[...] + _dot(ab, wout_ref[...])
    x1_ref[...] = x1
    h2_ref[...] = _rms(x1, gffn_ref[...]).astype(BF16)


def _even_mix(p, ctx_arr, ctx_spec, x, invc, wmix, mask, bcol, band, wpool, pscale, wout, gffn,
              *, tile, pctx, seq_tiles):
    n = x.shape[0]
    kb = band.shape[2]
    const2 = lambda i: (0, 0)
    const3 = lambda i: (0, 0, 0)
    return pl.pallas_call(
        functools.partial(_even_mix_kernel, tile=tile, pctx=pctx, seq_tiles=seq_tiles),
        grid=(n // tile,),
        in_specs=[
            pl.BlockSpec((tile, 3 * A_WIDTH), lambda i: (i, 0)),
            ctx_spec,
            pl.BlockSpec((tile, D_MODEL), lambda i: (i, 0)),
            pl.BlockSpec((tile, A_GROUPS), lambda i: (i, 0)),
            pl.BlockSpec((A_GROUPS, CHUNK, CHUNK), const3),
            pl.BlockSpec((CHUNK, CHUNK), const2),
            pl.BlockSpec((CHUNK, A_GROUPS), const2),
            pl.BlockSpec((A_GROUPS, CHUNK, kb), const3),
            pl.BlockSpec((A_GROUPS, LANES, LANES), const3),
            pl.BlockSpec((1, B_WIDTH), const2),
            pl.BlockSpec((D_MODEL, D_MODEL), const2),
            pl.BlockSpec((1, D_MODEL), const2),
        ],
        out_specs=[pl.BlockSpec((tile, D_MODEL), lambda i: (i, 0)),
                   pl.BlockSpec((tile, D_MODEL), lambda i: (i, 0))],
        out_shape=[jax.ShapeDtypeStruct((n, D_MODEL), F32),
                   jax.ShapeDtypeStruct((n, D_MODEL), BF16)],
        compiler_params=_cparams(("parallel",)),
        name="even_mix",
    )(p, ctx_arr, x, invc, wmix, mask, bcol, band, wpool, pscale, wout, gffn)


def _prompt_even_consts():
    r = np.arange(CHUNK)
    mask = (r[None, :] <= r[:, None]).astype(np.float32)
    j = np.arange(2 * CHUNK)
    band = np.stack([((j[None, :] <= CHUNK + r[:, None]) & (j[None, :] > CHUNK + r[:, None] - w))
                     for w in POOL_WINDOWS]).astype(np.float32)
    pos = np.arange(SEQ)
    invc = np.stack([1.0 / np.minimum(pos + 1, w) for w in POOL_WINDOWS], axis=1).astype(np.float32)
    return jnp.asarray(mask), jnp.asarray(band, dtype=BF16), jnp.asarray(invc)


def _sample_even_consts():
    ns = DEC_BATCH * DEC_SEQ
    r = np.arange(ns)
    rb, rt = r // DEC_SEQ, r % DEC_SEQ
    mask = ((rb[:, None] == rb[None, :]) & (rt[None, :] <= rt[:, None])).astype(np.float32)
    nst = DEC_BATCH * (POOL_BUF + 1)
    j = np.arange(nst + ns)
    jb = np.where(j < nst, j // (POOL_BUF + 1), (j - nst) // DEC_SEQ)
    je = np.where(j < nst, j % (POOL_BUF + 1) - 1, POOL_BUF + (j - nst) % DEC_SEQ)
    re = POOL_BUF + rt
    band = np.stack([((jb[None, :] == rb[:, None]) & (je[None, :] <= re[:, None])
                      & (je[None, :] > re[:, None] - w)) for w in POOL_WINDOWS]).astype(np.float32)
    pos = PAST_LEN + rt
    invc = np.stack([1.0 / np.minimum(pos + 1, w) for w in POOL_WINDOWS], axis=1).astype(np.float32)
    return jnp.asarray(mask), jnp.asarray(band, dtype=BF16), jnp.asarray(invc)


def _swiglu_kernel(h_ref, x_ref, wg_ref, wu_ref, wd_ref, o_ref, *, nsplit):
    h = h_ref[...]
    ff = wg_ref.shape[1]
    step = ff // nsplit
    acc = x_ref[...]
    for s in range(nsplit):
        g = _dot(h, wg_ref[:, s * step:(s + 1) * step])
        u = _dot(h, wu_ref[:, s * step:(s + 1) * step])
        a = (g * jax.nn.sigmoid(g) * u).astype(BF16)
        acc = acc + _dot(a, wd_ref[s * step:(s + 1) * step, :])
    o_ref[...] = acc


def _swiglu(h, x, wg, wu, wd, tm):
    n = x.shape[0]
    ff = wg.shape[1]
    single = pl.Buffered(1)
    return pl.pallas_call(
        functools.partial(_swiglu_kernel, nsplit=2),
        grid=(n // tm,),
        in_specs=[
            pl.BlockSpec((tm, D_MODEL), lambda i: (i, 0)),
            pl.BlockSpec((tm, D_MODEL), lambda i: (i, 0)),
            pl.BlockSpec((D_MODEL, ff), lambda i: (0, 0), pipeline_mode=single),
            pl.BlockSpec((D_MODEL, ff), lambda i: (0, 0), pipeline_mode=single),
            pl.BlockSpec((ff, D_MODEL), lambda i: (0, 0), pipeline_mode=single),
        ],
        out_specs=pl.BlockSpec((tm, D_MODEL), lambda i: (i, 0)),
        out_shape=jax.ShapeDtypeStruct((n, D_MODEL), F32),
        compiler_params=_cparams(("parallel",)),
        name="swiglu_dense",
    )(h, x, wg, wu, wd)


def _odd_in_kernel(x_ref, g_ref, w_ref, qg_ref, kg_ref, bd_ref,
                   q_ref, k_ref, v_ref, kb_ref, vb_ref, gb_ref, zc_ref):
    h = _rms(x_ref[...], g_ref[...])
    p = _dot(h.astype(BF16), w_ref[...])
    bd = bd_ref[...]
    q = p[:, 0:C_WIDTH]
    k = p[:, C_WIDTH:2 * C_WIDTH]
    v = p[:, 2 * C_WIDTH:3 * C_WIDTH]
    qn = q * lax.rsqrt(_group_mean_sq(q, bd, C_HD) + EPS) * qg_ref[...]
    kn = k * lax.rsqrt(_group_mean_sq(k, bd, C_HD) + EPS) * kg_ref[...]
    q_ref[...] = (qn * (C_HD ** -0.5 * LOG2E)).astype(BF16)
    k_ref[...] = kn
    kb_ref[...] = kn.astype(BF16)
    v_ref[...] = v
    vb_ref[...] = v.astype(BF16)
    gb_ref[...] = p[:, 3 * C_WIDTH:3 * C_WIDTH + D_WIDTH]
    zc_ref[...] = (p[:, 3 * C_WIDTH + D_WIDTH:3 * C_WIDTH + 2 * D_WIDTH]
                   * p[:, 3 * C_WIDTH + 2 * D_WIDTH:])


def _odd_in(x, g, w, qg, kg, tm):
    n = x.shape[0]
    nout = w.shape[1]
    bd = _block_diag_ones(C_WIDTH, C_HD)
    row = lambda i: (i, 0)
    const = lambda i: (0, 0)
    o512 = pl.BlockSpec((tm, C_WIDTH), row)
    return pl.pallas_call(
        _odd_in_kernel,
        grid=(n // tm,),
        in_specs=[
            pl.BlockSpec((tm, D_MODEL), row),
            pl.BlockSpec((1, D_MODEL), const),
            pl.BlockSpec((D_MODEL, nout), const),
            pl.BlockSpec((1, C_WIDTH), const),
            pl.BlockSpec((1, C_WIDTH), const),
            pl.BlockSpec((C_WIDTH, C_WIDTH), const),
        ],
        out_specs=[o512] * 7,
        out_shape=[jax.ShapeDtypeStruct((n, C_WIDTH), BF16),
                   jax.ShapeDtypeStruct((n, C_WIDTH), F32),
                   jax.ShapeDtypeStruct((n, C_WIDTH), F32),
                   jax.ShapeDtypeStruct((n, C_WIDTH), BF16),
                   jax.ShapeDtypeStruct((n, C_WIDTH), BF16),
                   jax.ShapeDtypeStruct((n, D_WIDTH), F32),
                   jax.ShapeDtypeStruct((n, D_WIDTH), F32)],
        compiler_params=_cparams(("parallel",)),
        name="odd_in",
    )(x, g, w, qg, kg, bd)


def _neg_abs(z):
    bits = lax.bitcast_convert_type(z, jnp.uint32) | jnp.uint32(0x80000000)
    return lax.bitcast_convert_type(bits, F32)


def _sb_block(z, carry, upper, valid):
    soft = jnp.maximum(z, 0.0) + jnp.log(1.0 + jnp.exp2(_neg_abs(z))) * LOG2E
    log_beta = z - soft
    if valid is not None:
        soft = jnp.where(valid, soft, 0.0)
    later = _dot(soft.astype(BF16), upper)
    w = jnp.exp2(log_beta - later - carry)
    if valid is not None:
        w = jnp.where(valid, w, 0.0)
    return w, carry + jnp.sum(soft, axis=1, keepdims=True)


def _upper_ones(n):
    return (lax.broadcasted_iota(jnp.int32, (n, n), 0)
            > lax.broadcasted_iota(jnp.int32, (n, n), 1)).astype(F32).astype(BF16)


def _attn_prompt_kernel(bias_ref, q_ref, k_ref, v_ref, upper_ref, o_ref, acc_ref, z_ref, lb_ref, later_ref,
                        *, tq):
    hp = pl.program_id(1)
    i = pl.program_id(2)
    qf = q_ref[...].astype(F32)
    lane_head = lax.broadcasted_iota(jnp.int32, (tq, LANES), 1) // C_HD
    qh = [jnp.where(lane_head == hh, qf, 0.0).astype(BF16) for hh in range(2)]
    bias = [bias_ref[hp * 2 + hh] * LOG2E for hh in range(2)]

    def stage_a(j):
        start = pl.multiple_of(j * tq, tq)
        kb = k_ref[pl.ds(start, tq), :]
        for hh in range(2):
            z_ref[hh] = _dot_nt(qh[hh], kb)

    def stage_b(diag):
        upper = upper_ref[...]
        tots = []
        for hh in range(2):
            z = z_ref[hh] + bias[hh]
            soft = jnp.maximum(z, 0.0) + jnp.log(1.0 + jnp.exp2(_neg_abs(z))) * LOG2E
            log_beta = z - soft
            if diag:
                valid = (lax.broadcasted_iota(jnp.int32, (tq, tq), 1)
                         < lax.broadcasted_iota(jnp.int32, (tq, tq), 0))
                soft = jnp.where(valid, soft, 0.0)
                log_beta = jnp.where(valid, log_beta, -jnp.inf)
            lb_ref[hh] = log_beta
            later_ref[hh] = _dot(soft.astype(BF16), upper)
            tots.append(jnp.sum(soft, axis=1, keepdims=True))
        return tuple(tots)

    def stage_c(j, carries, first):
        start = pl.multiple_of(j * tq, tq)
        vb = v_ref[pl.ds(start, tq), :]
        for hh in range(2):
            w = jnp.exp2(lb_ref[hh] - later_ref[hh] - carries[hh])
            pv = _dot(w.astype(BF16), vb)
            if first:
                acc_ref[hh] = pv
            else:
                acc_ref[hh] += pv

    zero = (jnp.zeros((tq, 1), F32),) * 2
    stage_a(i)
    tot_i = stage_b(True)

    @pl.when(i == 0)
    def _():
        stage_c(i, zero, True)

    @pl.when(i > 0)
    def _():
        stage_a(i - 1)

        def body(t, state):
            carries, tots = state
            jc = i - t
            stage_c(jc, carries, False)
            tots_new = stage_b(False)
            stage_a(jc - 2)
            return (tuple(c + s for c, s in zip(carries, tots)), tots_new)

        acc_ref[...] = jnp.zeros_like(acc_ref)
        carries, tots = lax.fori_loop(0, i - 1, body, (zero, tot_i))
        stage_c(1, carries, False)
        tot_0 = stage_b(False)
        del tot_0
        stage_c(0, tuple(c + s for c, s in zip(carries, tots)), False)

    o_ref[...] = jnp.where(lane_head == 0, acc_ref[0], acc_ref[1]).astype(BF16)


def _attn_prompt(q, kb, vb, bias, batch, seq):
    n = q.shape[0]
    tq = ATT_TQ
    nq = seq // tq
    r = np.arange(tq)
    upper = jnp.asarray((r[:, None] > r[None, :]).astype(np.float32), dtype=BF16)
    return pl.pallas_call(
        functools.partial(_attn_prompt_kernel, tq=tq),
        grid_spec=pltpu.PrefetchScalarGridSpec(
            num_scalar_prefetch=1,
            grid=(batch, C_HEADS // 2, nq),
            in_specs=[
                pl.BlockSpec((tq, LANES), lambda b, hp, i, bias: (b * nq + i, hp)),
                pl.BlockSpec((seq, LANES), lambda b, hp, i, bias: (b, hp)),
                pl.BlockSpec((seq, LANES), lambda b, hp, i, bias: (b, hp)),
                pl.BlockSpec((tq, tq), lambda b, hp, i, bias: (0, 0)),
            ],
            out_specs=pl.BlockSpec((tq, LANES), lambda b, hp, i, bias: (b * nq + i, hp)),
            scratch_shapes=[pltpu.VMEM((2, tq, LANES), F32),
                            pltpu.VMEM((2, tq, tq), F32),
                            pltpu.VMEM((2, tq, tq), F32),
                            pltpu.VMEM((2, tq, tq), F32)],
        ),
        out_shape=jax.ShapeDtypeStruct((n, C_WIDTH), BF16),
        compiler_params=_cparams(("parallel", "parallel", "arbitrary")),
        name="attn_prompt",
    )(bias, q, kb, vb, upper)


def _attn_sample_kernel(pt_ref, q_ref, kn_ref, vn_ref, bias_ref, *rest, layer, nsteps):
    del pt_ref, layer
    g_pages = PAGES_PER_STEP
    k_refs = rest[:g_pages]
    v_refs = rest[g_pages:2 * g_pages]
    o_ref, acc_ref, carry_ref, qb_ref = rest[2 * g_pages:]
    j = pl.program_id(1)
    nrow = DEC_SEQ * C_HEADS
    row = lax.broadcasted_iota(jnp.int32, (nrow, C_WIDTH), 0)
    head_lane = (lax.broadcasted_iota(jnp.int32, (nrow, C_WIDTH), 1) // C_HD) == (row % C_HEADS)
    upper = _upper_ones(PAGE_SIZE)
    bias = bias_ref[...] * LOG2E

    @pl.when(j == 0)
    def _():
        qf = q_ref[...]
        qrep = jnp.concatenate(
            [jnp.broadcast_to(qf[t:t + 1, :], (C_HEADS, C_WIDTH)) for t in range(DEC_SEQ)], axis=0)
        qb = jnp.where(head_lane, qrep, 0.0).astype(BF16)
        qb_ref[...] = qb
        pad = jnp.zeros((PAGE_SIZE - 8, C_WIDTH), F32)
        kn = jnp.concatenate([kn_ref[...], pad], axis=0).astype(BF16)
        vn = jnp.concatenate([vn_ref[...], pad], axis=0).astype(BF16)
        z = _dot_nt(qb, kn) + bias
        kcol = lax.broadcasted_iota(jnp.int32, (nrow, PAGE_SIZE), 1)
        qrow = lax.broadcasted_iota(jnp.int32, (nrow, PAGE_SIZE), 0) // C_HEADS
        w, carry = _sb_block(z, jnp.zeros((nrow, 1), F32), upper, kcol < qrow)
        acc_ref[...] = _dot(w.astype(BF16), vn)
        carry_ref[...] = carry

    qb = qb_ref[...]
    k_all = jnp.concatenate([k_refs[g][...].astype(BF16) for g in range(g_pages)], axis=1)
    v_all = jnp.concatenate([v_refs[g][...].astype(BF16) for g in range(g_pages)], axis=1)
    z_all = _dot(qb, k_all) + bias
    soft_all = jnp.maximum(z_all, 0.0) + jnp.log(1.0 + jnp.exp2(_neg_abs(z_all))) * LOG2E
    lb_all = z_all - soft_all
    softs = [soft_all[:, g * PAGE_SIZE:(g + 1) * PAGE_SIZE] for g in range(g_pages)]
    laters = [_dot(s.astype(BF16), upper) for s in softs]
    carry = carry_ref[...]
    ws = [None] * g_pages
    for g in range(g_pages - 1, -1, -1):
        lb = lb_all[:, g * PAGE_SIZE:(g + 1) * PAGE_SIZE]
        ws[g] = jnp.exp2(lb - laters[g] - carry).astype(BF16)
        carry = carry + jnp.sum(softs[g], axis=1, keepdims=True)
    acc = acc_ref[...] + _dot_nt(jnp.concatenate(ws, axis=1), v_all)
    acc_ref[...] = acc
    carry_ref[...] = carry

    @pl.when(j == nsteps - 1)
    def _():
        masked = jnp.where(head_lane, acc, 0.0)
        orow = lax.broadcasted_iota(jnp.int32, (8, C_WIDTH), 0)
        out = jnp.zeros((8, C_WIDTH), F32)
        for t in range(DEC_SEQ):
            tok = jnp.sum(masked[t * C_HEADS:(t + 1) * C_HEADS, :], axis=0, keepdims=True)
            out = jnp.where(orow == t, tok, out)
        o_ref[...] = out


def _attn_sample(q, kn, vn, bias_col, cache_k, cache_v, page_table, layer):
    g_pages = PAGES_PER_STEP
    nsteps = N_PAGES // g_pages
    nrow = DEC_SEQ * C_HEADS
    pad8 = lambda a: jnp.pad(a.astype(F32), ((0, 0), (0, 8 - DEC_SEQ), (0, 0)))
    q, kn, vn = pad8(q), pad8(kn), pad8(vn)

    def page_spec(g):
        def imap(b, j, pt):
            return (layer, pt[b * N_PAGES + (nsteps - 1 - j) * g_pages + g], 0, 0)
        return pl.BlockSpec((None, None, C_WIDTH, PAGE_SIZE), imap)

    seq3 = lambda b, j, pt: (b, 0, 0)
    out = pl.pallas_call(
        functools.partial(_attn_sample_kernel, layer=layer, nsteps=nsteps),
        grid_spec=pltpu.PrefetchScalarGridSpec(
            num_scalar_prefetch=1,
            grid=(DEC_BATCH, nsteps),
            in_specs=[pl.BlockSpec((None, 8, C_WIDTH), seq3),
                      pl.BlockSpec((None, 8, C_WIDTH), seq3),
                      pl.BlockSpec((None, 8, C_WIDTH), seq3),
                      pl.BlockSpec((nrow, 1), lambda b, j, pt: (0, 0))]
                     + [page_spec(g) for g in range(g_pages)]
                     + [page_spec(g) for g in range(g_pages)],
            out_specs=pl.BlockSpec((None, 8, C_WIDTH), seq3),
            scratch_shapes=[pltpu.VMEM((nrow, C_WIDTH), F32),
                            pltpu.VMEM((nrow, 1), F32),
                            pltpu.VMEM((nrow, C_WIDTH), BF16)],
        ),
        out_shape=jax.ShapeDtypeStruct((DEC_BATCH, 8, C_WIDTH), F32),
        compiler_params=_cparams(("parallel", "arbitrary")),
        name="attn_sample",
    )(page_table.reshape(-1), q, kn, vn, bias_col, *([cache_k] * g_pages), *([cache_v] * g_pages))
    return out[:, :DEC_SEQ].astype(BF16)


def _odd_tail(att, d, x, wout_ref, gffn_ref, rhi_ref, rlo_ref, x1_ref, h2_ref, route_ref):
    ad = jnp.concatenate([att, d.astype(BF16)], axis=1)
    x1 = x + _dot(ad, wout_ref[...])
    x1_ref[...] = x1
    h2 = _rms(x1, gffn_ref[...])
    h2_ref[...] = h2
    hhi, hlo = _split2(h2)
    rhi = rhi_ref[...]
    logits = _dot(hhi, rhi) + _dot(hlo, rhi) + _dot(hhi, rlo_ref[...])
    lane = lax.broadcasted_iota(jnp.int32, logits.shape, 1).astype(F32)
    neg = jnp.float32(-jnp.inf)
    l1 = jnp.where(lane < N_EXPERTS, logits, neg)
    v1 = jnp.max(l1, axis=1, keepdims=True)
    e1 = jnp.min(jnp.where(l1 == v1, lane, float(LANES)), axis=1, keepdims=True)
    l2 = jnp.where(lane == e1, neg, l1)
    v2 = jnp.max(l2, axis=1, keepdims=True)
    e2 = jnp.min(jnp.where(l2 == v2, lane, float(LANES)), axis=1, keepdims=True)
    ex = jnp.exp(v2 - v1)
    g1 = 1.0 / (1.0 + ex)
    g2 = ex / (1.0 + ex)
    route_ref[...] = jnp.where(
        lane == 0, e1, jnp.where(lane == 1, e2, jnp.where(lane == 2, g1, jnp.where(lane == 3, g2, 0.0))))


def _odd_mix_prompt_kernel(att_ref, gb_ref, zc_ref, halo_ref, x_ref, cw_ref, wout_ref, gffn_ref,
                           rhi_ref, rlo_ref, x1_ref, h2_ref, route_ref, *, tile, seq_tiles):
    halo = halo_ref[...]
    first = (pl.program_id(0) % seq_tiles) == 0
    halo = jnp.where(first, 0.0, halo)
    zc = zc_ref[...]
    ext = jnp.concatenate([halo, zc], axis=0)
    hs = halo.shape[0]
    z1 = pltpu.roll(ext, 1, axis=0)[hs:, :]
    z2 = pltpu.roll(ext, 2, axis=0)[hs:, :]
    conv = z2 * cw_ref[0:1, :] + z1 * cw_ref[1:2, :] + zc * cw_ref[2:3, :]
    d = gb_ref[...] * conv
    _odd_tail(att_ref[...], d, x_ref[...], wout_ref, gffn_ref, rhi_ref, rlo_ref,
              x1_ref, h2_ref, route_ref)


def _odd_mix_sample_kernel(att_ref, gb_ref, ext_ref, sel_ref, x_ref, cw_ref, wout_ref, gffn_ref,
                           rhi_ref, rlo_ref, x1_ref, h2_ref, route_ref):
    parts = _split3(ext_ref[...])
    conv = None
    for jj in range(CONV_W):
        sel = sel_ref[jj]
        shifted = _dot(sel, parts[0]) + _dot(sel, parts[1]) + _dot(sel, parts[2])
        term = shifted * cw_ref[jj:jj + 1, :]
        conv = term if conv is None else conv + term
    d = gb_ref[...] * conv
    _odd_tail(att_ref[...], d, x_ref[...], wout_ref, gffn_ref, rhi_ref, rlo_ref,
              x1_ref, h2_ref, route_ref)


def _odd_mix_out(n, tile):
    row = lambda i: (i, 0)
    specs = [pl.BlockSpec((tile, D_MODEL), row), pl.BlockSpec((tile, D_MODEL), row),
             pl.BlockSpec((tile, LANES), row)]
    shapes = [jax.ShapeDtypeStruct((n, D_MODEL), F32), jax.ShapeDtypeStruct((n, D_MODEL), F32),
              jax.ShapeDtypeStruct((n, LANES), F32)]
    return specs, shapes


def _odd_mix_prompt(att, gb, zc, x, cw, wout, gffn, rhi, rlo, tile, seq):
    n = x.shape[0]
    row = lambda i: (i, 0)
    const = lambda i: (0, 0)
    hb = tile // 8
    out_specs, out_shape = _odd_mix_out(n, tile)
    return pl.pallas_call(
        functools.partial(_odd_mix_prompt_kernel, tile=tile, seq_tiles=seq // tile),
        grid=(n // tile,),
        in_specs=[
            pl.BlockSpec((tile, C_WIDTH), row),
            pl.BlockSpec((tile, D_WIDTH), row),
            pl.BlockSpec((tile, D_WIDTH), row),
            pl.BlockSpec((8, D_WIDTH), lambda i: (jnp.maximum(i * hb - 1, 0), 0)),
            pl.BlockSpec((tile, D_MODEL), row),
            pl.BlockSpec((CONV_W, D_WIDTH), const),
            pl.BlockSpec((D_MODEL, D_MODEL), const),
            pl.BlockSpec((1, D_MODEL), const),
            pl.BlockSpec((D_MODEL, LANES), const),
            pl.BlockSpec((D_MODEL, LANES), const),
        ],
        out_specs=out_specs,
        out_shape=out_shape,
        compiler_params=_cparams(("parallel",)),
        name="odd_mix_prompt",
    )(att, gb, zc, zc, x, cw, wout, gffn, rhi, rlo)


def _odd_mix_sample(att, gb, ext, sel, x, cw, wout, gffn, rhi, rlo):
    n = x.shape[0]
    const = lambda i: (0, 0)
    out_specs, out_shape = _odd_mix_out(n, n)
    return pl.pallas_call(
        _odd_mix_sample_kernel,
        grid=(1,),
        in_specs=[
            pl.BlockSpec((n, C_WIDTH), const),
            pl.BlockSpec((n, D_WIDTH), const),
            pl.BlockSpec(ext.shape, const),
            pl.BlockSpec(sel.shape, lambda i: (0, 0, 0)),
            pl.BlockSpec((n, D_MODEL), const),
            pl.BlockSpec((CONV_W, D_WIDTH), const),
            pl.BlockSpec((D_MODEL, D_MODEL), const),
            pl.BlockSpec((1, D_MODEL), const),
            pl.BlockSpec((D_MODEL, LANES), const),
            pl.BlockSpec((D_MODEL, LANES), const),
        ],
        out_specs=out_specs,
        out_shape=out_shape,
        compiler_params=_cparams(("arbitrary",)),
        name="odd_mix_sample",
    )(att, gb, ext, sel, x, cw, wout, gffn, rhi, rlo)


def _sample_conv_select():
    ns = DEC_BATCH * DEC_SEQ
    r = np.arange(ns)
    rb, rt = r // DEC_SEQ, r % DEC_SEQ
    c = np.arange(DEC_BATCH * 8)
    sel = np.stack([(c[None, :] == (rb * 8 + rt + jj)[:, None]) for jj in range(CONV_W)])
    return jnp.asarray(sel.astype(np.float32), dtype=BF16)


def _row_copy(src_hbm, src_row, dst, dst_row, sem):
    return pltpu.make_async_copy(src_hbm.at[pl.ds(src_row, 1), :], dst.at[pl.ds(dst_row, 1), :], sem)


def _start_rows(idx_ref, src_hbm, dst, sem, nrows):
    def start(r, c):
        _row_copy(src_hbm, idx_ref[0, 0, r], dst, r, sem).start()
        return c

    lax.fori_loop(0, nrows, start, 0, unroll=8)


def _wait_rows(src_hbm, dst, sem, nrows):
    pltpu.make_async_copy(src_hbm.at[pl.ds(0, nrows), :], dst, sem).wait()


def _moe_gather_kernel(cur_ref, nxt_ref, h_hbm, o_ref, buf, sem, *, tm, ntiles):
    t = pl.program_id(0)
    slot = t % 2

    @pl.when(t == 0)
    def _():
        _start_rows(cur_ref, h_hbm, buf.at[0], sem.at[0], tm)

    @pl.when(t + 1 < ntiles)
    def _():
        _start_rows(nxt_ref, h_hbm, buf.at[1 - slot], sem.at[1 - slot], tm)

    _wait_rows(h_hbm, buf.at[slot], sem.at[slot], tm)
    o_ref[...] = buf[slot].astype(BF16)


def _moe_gather(src, h, tm):
    ntiles = src.shape[0]
    smem = lambda imap: pl.BlockSpec((1, 1, tm), imap, memory_space=pltpu.SMEM)
    return pl.pallas_call(
        functools.partial(_moe_gather_kernel, tm=tm, ntiles=ntiles),
        grid=(ntiles,),
        in_specs=[smem(lambda t: (t, 0, 0)),
                  smem(lambda t: (jnp.minimum(t + 1, ntiles - 1), 0, 0)),
                  pl.BlockSpec(memory_space=pl.ANY)],
        out_specs=pl.BlockSpec((tm, D_MODEL), lambda t: (t, 0)),
        out_shape=jax.ShapeDtypeStruct((ntiles * tm, D_MODEL), BF16),
        scratch_shapes=[pltpu.VMEM((2, tm, D_MODEL), F32), pltpu.SemaphoreType.DMA((2,))],
        compiler_params=_cparams(("arbitrary",)),
        name="moe_gather",
    )(src, src, h)


def _moe_ffn_kernel(te_ref, nused_ref, x_ref, wg_ref, wu_ref, wd_ref, o_ref, acc_ref, *, nf):
    del te_ref
    t = pl.program_id(0)
    f = pl.program_id(1)
    used = t < nused_ref[0]

    @pl.when(f == 0)
    def _():
        acc_ref[...] = jnp.zeros_like(acc_ref)

    @pl.when(used)
    def _():
        x = x_ref[...]
        g = _dot(x, wg_ref[...].astype(BF16))
        u = _dot(x, wu_ref[...].astype(BF16))
        a = (g * jax.nn.sigmoid(g) * u).astype(BF16)
        acc_ref[...] += _dot(a, wd_ref[...].astype(BF16))

    @pl.when(f == nf - 1)
    def _():
        o_ref[...] = acc_ref[...]


def _moe_ffn(tile_expert, nused, xs, wg, wu, wd, layer, tm, tf):
    p = xs.shape[0]
    ff = wg.shape[3]
    nf = ff // tf
    return pl.pallas_call(
        functools.partial(_moe_ffn_kernel, nf=nf),
        grid_spec=pltpu.PrefetchScalarGridSpec(
            num_scalar_prefetch=2,
            grid=(p // tm, nf),
            in_specs=[
                pl.BlockSpec((tm, D_MODEL), lambda t, f, te, nu: (t, 0)),
                pl.BlockSpec((None, None, D_MODEL, tf), lambda t, f, te, nu: (layer, te[t], 0, f)),
                pl.BlockSpec((None, None, D_MODEL, tf), lambda t, f, te, nu: (layer, te[t], 0, f)),
                pl.BlockSpec((None, None, tf, D_MODEL), lambda t, f, te, nu: (layer, te[t], f, 0)),
            ],
            out_specs=pl.BlockSpec((tm, D_MODEL), lambda t, f, te, nu: (t, 0)),
            scratch_shapes=[pltpu.VMEM((tm, D_MODEL), F32)],
        ),
        out_shape=jax.ShapeDtypeStruct((p, D_MODEL), F32),
        compiler_params=_cparams(("parallel", "arbitrary")),
        name="moe_ffn",
    )(tile_expert, nused, xs, wg, wu, wd)


def _moe_combine_kernel(c0_ref, c1_ref, n0_ref, n1_ref, x_ref, route_ref, y_hbm, o_ref, buf, sem,
                        *, tile, ntiles):
    t = pl.program_id(0)
    slot = t % 2

    def start(i0_ref, i1_ref, s):
        _start_rows(i0_ref, y_hbm, buf.at[s, 0], sem.at[s], tile)
        _start_rows(i1_ref, y_hbm, buf.at[s, 1], sem.at[s], tile)

    @pl.when(t == 0)
    def _():
        start(c0_ref, c1_ref, 0)

    @pl.when(t + 1 < ntiles)
    def _():
        start(n0_ref, n1_ref, 1 - slot)

    _wait_rows(y_hbm, buf.at[slot, 0], sem.at[slot], tile)
    _wait_rows(y_hbm, buf.at[slot, 1], sem.at[slot], tile)
    g1 = route_ref[:, 2:3]
    g2 = route_ref[:, 3:4]
    o_ref[...] = x_ref[...] + g1 * buf[slot, 0] + g2 * buf[slot, 1]


def _moe_combine(pos0, pos1, x, route, ys, tile):
    n = x.shape[0]
    ntiles = n // tile
    row = lambda t: (t, 0)
    cur = lambda t: (t, 0, 0)
    nxt = lambda t: (jnp.minimum(t + 1, ntiles - 1), 0, 0)
    smem = lambda imap: pl.BlockSpec((1, 1, tile), imap, memory_space=pltpu.SMEM)
    return pl.pallas_call(
        functools.partial(_moe_combine_kernel, tile=tile, ntiles=ntiles),
        grid=(ntiles,),
        in_specs=[smem(cur), smem(cur), smem(nxt), smem(nxt),
                  pl.BlockSpec((tile, D_MODEL), row),
                  pl.BlockSpec((tile, LANES), row),
                  pl.BlockSpec(memory_space=pl.ANY)],
        out_specs=pl.BlockSpec((tile, D_MODEL), row),
        out_shape=jax.ShapeDtypeStruct((n, D_MODEL), F32),
        scratch_shapes=[pltpu.VMEM((2, TOP_K, tile, D_MODEL), F32), pltpu.SemaphoreType.DMA((2,))],
        compiler_params=_cparams(("arbitrary",)),
        name="moe_combine",
    )(pos0, pos1, pos0, pos1, x, route, ys)


def _moe(h2, x1, route, wg, wu, wd, layer):
    n = h2.shape[0]
    tm = MOE_TM
    ntiles = (TOP_K * n + N_EXPERTS * (tm - 1)) // tm + 1
    experts = route[:, 0:TOP_K].astype(jnp.int32)
    flat = experts.reshape(-1)
    onehot = (flat[:, None] == jnp.arange(N_EXPERTS, dtype=jnp.int32)[None, :]).astype(jnp.int32)
    csum = jnp.cumsum(onehot, axis=0)
    counts = csum[-1]
    rank = jnp.sum((csum - onehot) * onehot, axis=1)
    padded = ((counts + tm - 1) // tm) * tm
    ends = jnp.cumsum(padded)
    offs = ends - padded
    dest = offs[flat] + rank
    token = jnp.arange(TOP_K * n, dtype=jnp.int32) // TOP_K
    src = jnp.zeros((ntiles * tm,), jnp.int32).at[dest].set(token)
    tile_start = jnp.arange(ntiles, dtype=jnp.int32) * tm
    tile_expert = jnp.minimum(jnp.sum((ends[None, :] <= tile_start[:, None]).astype(jnp.int32), axis=1),
                              N_EXPERTS - 1)
    nused = (ends[-1] // tm).astype(jnp.int32).reshape(1)
    xs = _moe_gather(src.reshape(ntiles * tm // MOE_GATHER_ROWS, 1, MOE_GATHER_ROWS), h2, MOE_GATHER_ROWS)
    ys = _moe_ffn(tile_expert, nused, xs, wg, wu, wd, layer, tm, MOE_TF)
    ctile = 128
    pos = dest.reshape(n, TOP_K)
    pos0 = pos[:, 0].reshape(n // ctile, 1, ctile)
    pos1 = pos[:, 1].reshape(n // ctile, 1, ctile)
    return _moe_combine(pos0, pos1, x1, route, ys, ctile)


def kernel(x_prompt, x_sample, cache_k, cache_v, page_table, state_pool, state_conv, norm_mix, norm_ffn,
           w_in_even, w_out_even, v_norm_g, w_spatial, b_spatial, w_pool, pool_scale,
           w_in_odd, w_out_odd, q_norm_g, k_norm_g, sb_bias, conv_w,
           w_gate_dense, w_up_dense, w_down_dense, w_router, w_gate_exp, w_up_exp, w_down_exp):
    bp, seq, _ = x_prompt.shape
    bs, ts, _ = x_sample.shape
    np_rows = bp * seq
    ns_rows = bs * ts
    xp = x_prompt.reshape(np_rows, D_MODEL)
    xs = x_sample.reshape(ns_rows, D_MODEL)
    n_pool = cache_k.shape[1]
    ck = jnp.transpose(cache_k, (0, 1, 3, 4, 2)).reshape(cache_k.shape[0], n_pool, C_WIDTH, PAGE_SIZE)
    cv = jnp.transpose(cache_v, (0, 1, 3, 4, 2)).reshape(cache_v.shape[0], n_pool, C_WIDTH, PAGE_SIZE)

    mask_p, band_p, invc_p = _prompt_even_consts()
    invc_p = jnp.tile(invc_p, (bp, 1))
    mask_s, band_s, invc_s = _sample_even_consts()
    sel_s = _sample_conv_select()
    tile = ROW_TILE

    kp_l, vp_l, ks_l, vs_l = [], [], [], []
    poolp_l, pools_l, convp_l, convs_l, chunkv_l = [], [], [], [], []
    for layer in range(DEPTH):
        i = layer // 2
        g_mix = norm_mix[layer].reshape(1, D_MODEL)
        g_ffn = norm_ffn[layer].reshape(1, D_MODEL)
        if layer % 2 == 0:
            w_in = w_in_even[i].astype(BF16)
            w_out = w_out_even[i].astype(BF16)
            vg = v_norm_g[i].reshape(1, A_WIDTH)
            wpool = w_pool[i].astype(BF16)
            pscale = pool_scale[i].reshape(1, B_WIDTH)
            pp = _even_in(xp, g_mix, w_in, vg, tile)
            ctx_spec = pl.BlockSpec(
                (CHUNK, B_WIDTH), lambda t: (jnp.maximum(t * (tile // CHUNK) - 1, 0), 2))
            xp1, hp2 = _even_mix(pp, pp, ctx_spec, xp, invc_p, w_spatial[i], mask_p,
                                 b_spatial[i].T, band_p, wpool, pscale, w_out, g_ffn,
                                 tile=tile, pctx=CHUNK, seq_tiles=seq // tile)
            poolp_l.append(pp.reshape(bp, seq, 3 * A_WIDTH)[:, seq - POOL_BUF:, 2 * A_WIDTH:])
            ps = _even_in(xs, g_mix, w_in, vg, ns_rows)
            state = jnp.pad(state_pool[i], ((0, 0), (1, 0), (0, 0))).reshape(bs * (POOL_BUF + 1), B_WIDTH)
            wmix_s = jnp.tile(w_spatial[i][:, :ts, :ts], (1, bs, bs))
            bcol_s = jnp.tile(b_spatial[i][:, :ts], (1, bs)).T
            ctx_spec_s = pl.BlockSpec(state.shape, lambda t: (0, 0))
            xs1, hs2 = _even_mix(ps, state, ctx_spec_s, xs, invc_s, wmix_s, mask_s, bcol_s, band_s,
                                 wpool, pscale, w_out, g_ffn,
                                 tile=ns_rows, pctx=state.shape[0], seq_tiles=0)
            z_s = ps[:, 2 * A_WIDTH:].reshape(bs, ts, B_WIDTH)
            pools_l.append(jnp.concatenate([state_pool[i], z_s], axis=1)[:, -POOL_BUF:])
            chunkv_l.append(ps[:, A_WIDTH:2 * A_WIDTH].reshape(bs, ts, A_WIDTH))
            wg = w_gate_dense[i].astype(BF16)
            wu = w_up_dense[i].astype(BF16)
            wd = w_down_dense[i].astype(BF16)
            xp = _swiglu(hp2, xp1, wg, wu, wd, tile)
            xs = _swiglu(hs2, xs1, wg, wu, wd, ns_rows)
        else:
            w_in = w_in_odd[i].astype(BF16)
            w_out = w_out_odd[i].astype(BF16)
            qg = jnp.tile(q_norm_g[i], C_HEADS).reshape(1, C_WIDTH)
            kg = jnp.tile(k_norm_g[i], C_HEADS).reshape(1, C_WIDTH)
            router = jnp.pad(w_router[i], ((0, 0), (0, LANES - N_EXPERTS)))
            rhi = router.astype(BF16)
            rlo = (router - rhi.astype(F32)).astype(BF16)
            cw = conv_w[i]
            q_p, k_p, v_p, kb_p, vb_p, gb_p, zc_p = _odd_in(xp, g_mix, w_in, qg, kg, tile)
            att_p = _attn_prompt(q_p, kb_p, vb_p, sb_bias[i], bp, seq)
            xp1, hp2, route_p = _odd_mix_prompt(att_p, gb_p, zc_p, xp, cw, w_out, g_ffn, rhi, rlo,
                                                tile, seq)
            kp_l.append(k_p.reshape(bp, seq, C_HEADS, C_HD))
            vp_l.append(v_p.reshape(bp, seq, C_HEADS, C_HD))
            convp_l.append(zc_p.reshape(bp, seq, D_WIDTH)[:, seq - (CONV_W - 1):])
            q_s, k_s, v_s, _, _, gb_s, zc_s = _odd_in(xs, g_mix, w_in, qg, kg, ns_rows)
            bias_col = jnp.tile(sb_bias[i], ts).reshape(ts * C_HEADS, 1)
            att_s = _attn_sample(q_s.reshape(bs, ts, C_WIDTH), k_s.reshape(bs, ts, C_WIDTH),
                                 v_s.reshape(bs, ts, C_WIDTH), bias_col, ck, cv, page_table, i)
            zc_s3 = zc_s.reshape(bs, ts, D_WIDTH)
            ext = jnp.concatenate(
                [state_conv[i], zc_s3, jnp.zeros((bs, 8 - (CONV_W - 1) - ts, D_WIDTH), F32)], axis=1)
            xs1, hs2, route_s = _odd_mix_sample(att_s.reshape(ns_rows, C_WIDTH), gb_s,
                                                ext.reshape(bs * 8, D_WIDTH), sel_s, xs, cw, w_out,
                                                g_ffn, rhi, rlo)
            ks_l.append(k_s.reshape(bs, ts, C_HEADS, C_HD))
            vs_l.append(v_s.reshape(bs, ts, C_HEADS, C_HD))
            convs_l.append(zc_s3[:, ts - (CONV_W - 1):])
            h2 = jnp.concatenate([hp2, hs2], axis=0)
            x1 = jnp.concatenate([xp1, xs1], axis=0)
            route = jnp.concatenate([route_p, route_s], axis=0)
            x2 = _moe(h2, x1, route, w_gate_exp, w_up_exp, w_down_exp, i)
            xp = x2[:np_rows]
            xs = x2[np_rows:]
    return (xp.reshape(bp, seq, D_MODEL), xs.reshape(bs, ts, D_MODEL),
            jnp.stack(kp_l), jnp.stack(vp_l), jnp.stack(ks_l), jnp.stack(vs_l),
            jnp.stack(poolp_l), jnp.stack(pools_l), jnp.stack(convp_l), jnp.stack(convs_l),
            jnp.stack(chunkv_l))
```
